```python
import jax, jax.numpy as jnp
from jax import lax
import numpy as np

D_MODEL = 1024
BATCH = 4
SEQ = 4096
DEPTH = 2
DEC_BATCH = 32
DEC_SEQ = 2048
PAST_LEN = 128

N_MEM = 256
XA_HEADS = 4
XA_HEAD_DIM = D_MODEL // XA_HEADS
CONV_WIDTH = 31
D_CONV = D_MODEL // 2
D_FOURIER = D_MODEL - D_CONV
FOURIER_GROUPS = 4
FOURIER_GROUP_DIM = D_FOURIER // FOURIER_GROUPS
POOL_WINDOWS = (2, 4, 8, 16)
POOL_GROUPS = len(POOL_WINDOWS)
POOL_GROUP_DIM = D_MODEL // POOL_GROUPS
D_FF = ((8 * D_MODEL + 3 * 256 - 1) // (3 * 256)) * 256
N_EVEN = (DEPTH + 1) // 2
N_ODD = DEPTH // 2
RMS_EPS = 1e-6
LN_EPS = 1e-5

kernel_name = 'hybrid_conv_fourier_pool_encoder'


def rms_norm(x, g):
    xf = x.astype(jnp.float32)
    y = xf * lax.rsqrt(jnp.mean(xf * xf, axis=-1, keepdims=True) + RMS_EPS)
    return (y * g.astype(jnp.float32)).astype(x.dtype)


def layer_norm(x, g, b):
    xf = x.astype(jnp.float32)
    mu = jnp.mean(xf, axis=-1, keepdims=True)
    xc = xf - mu
    y = xc * lax.rsqrt(jnp.mean(xc * xc, axis=-1, keepdims=True) + LN_EPS)
    return (y * g.astype(jnp.float32) + b.astype(jnp.float32)).astype(x.dtype)


def conformer_conv(u_val, u_gate, w_dw, b_dw, ln_g, ln_b):
    h = u_val * jax.nn.sigmoid(u_gate)
    h = lax.conv_general_dilated(
        h, w_dw[:, None, :], window_strides=(1,),
        padding=[(CONV_WIDTH // 2, CONV_WIDTH // 2)],
        dimension_numbers=('NWC', 'WIO', 'NWC'),
        feature_group_count=D_CONV) + b_dw
    return jax.nn.silu(layer_norm(h, ln_g, ln_b))


def fourier_mix(u):
    b_, s_, _ = u.shape
    ug = u.reshape(b_, s_, FOURIER_GROUPS, FOURIER_GROUP_DIM).astype(jnp.float32)
    f = jnp.fft.fftn(ug, axes=(1, 3), norm='ortho').real
    return f.reshape(b_, s_, D_FOURIER).astype(u.dtype)


def multiscale_pool(h, w_grp, scale):
    b_, s_, _ = h.shape
    hf = h.astype(jnp.float32)
    csum = jnp.concatenate([jnp.zeros_like(hf[:, :1]), jnp.cumsum(hf, axis=1)], axis=1)
    pos = jnp.arange(s_)
    outs = []
    for g, w in enumerate(POOL_WINDOWS):
        c0 = g * POOL_GROUP_DIM
        c = csum[:, :, c0:c0 + POOL_GROUP_DIM]
        lo = jnp.clip(pos - w // 2, 0, s_)
        hi = jnp.clip(pos + w - w // 2, 0, s_)
        cnt = (hi - lo).astype(jnp.float32)[None, :, None]
        outs.append((c[:, hi] - c[:, lo]) / cnt - hf[:, :, c0:c0 + POOL_GROUP_DIM])
    p = jnp.stack(outs, axis=2).astype(h.dtype)
    y = jnp.einsum('bsgc,gcd->bsgd', p, w_grp).reshape(b_, s_, D_MODEL)
    return y * scale


def memory_cross_attn(h, mem_n, wq, wkv, wo):
    b_, s_, _ = h.shape
    m_ = mem_n.shape[1]
    q = (h @ wq).reshape(b_, s_, XA_HEADS, XA_HEAD_DIM)
    kv = (mem_n @ wkv).reshape(b_, m_, 2, XA_HEADS, XA_HEAD_DIM)
    k, v = kv[:, :, 0], kv[:, :, 1]
    scores = jnp.einsum('bshd,bmhd->bhsm', q, k).astype(jnp.float32) * (XA_HEAD_DIM ** -0.5)
    probs = jax.nn.softmax(scores, axis=-1).astype(v.dtype)
    o = jnp.einsum('bhsm,bmhd->bshd', probs, v).reshape(b_, s_, D_MODEL)
    return o @ wo


def swiglu(h, w_gate_up, w_down):
    gu = h @ w_gate_up
    g, u = gu[..., :D_FF], gu[..., D_FF:]
    return (jax.nn.silu(g) * u) @ w_down


def trunk(x, mem, norm_mix, w_in_even, conv_w, conv_b, conv_ln_g, conv_ln_b, w_out_even,
          w_pool, pool_scale, norm_xa, norm_mem, xa_wq, xa_wkv, xa_wo, norm_ffn,
          ffn_w_gate_up, ffn_w_down, norm_final):
    for l in range(DEPTH):
        h = rms_norm(x, norm_mix[l])
        if l % 2 == 0:
            e = l // 2
            u = h @ w_in_even[e]
            a = conformer_conv(u[..., :D_CONV], u[..., D_CONV:2 * D_CONV],
                               conv_w[e], conv_b[e], conv_ln_g[e], conv_ln_b[e])
            f = fourier_mix(u[..., 2 * D_CONV:])
            x = x + jnp.concatenate([a, f], axis=-1) @ w_out_even[e]
        else:
            o = l // 2
            x = x + multiscale_pool(h, w_pool[o], pool_scale[o])
        h = rms_norm(x, norm_xa[l])
        m = rms_norm(mem, norm_mem[l])
        x = x + memory_cross_attn(h, m, xa_wq[l], xa_wkv[l], xa_wo[l])
        h = rms_norm(x, norm_ffn[l])
        x = x + swiglu(h, ffn_w_gate_up[l], ffn_w_down[l])
    return rms_norm(x, norm_final)


def setup_inputs(seed: int = 0) -> dict:
    key = jax.random.key(seed)
    ks = jax.random.split(key, 24)
    f32 = jnp.float32
    nrm = lambda k, shape, s: jax.random.normal(k, shape, f32) * s
    gain = lambda k, shape: 1.0 + 0.05 * jax.random.normal(k, shape, f32)
    d_in_even = 2 * D_CONV + D_FOURIER
    return {
        'x_prompt': nrm(ks[0], (BATCH, SEQ, D_MODEL), 1.0),
        'x_sample': nrm(ks[1], (DEC_BATCH, DEC_SEQ, D_MODEL), 1.0),
        'mem_prompt': nrm(ks[2], (BATCH, N_MEM, D_MODEL), 1.0),
        'mem_sample': nrm(ks[3], (DEC_BATCH, N_MEM, D_MODEL), 1.0),
        'norm_mix': gain(ks[4], (DEPTH, D_MODEL)),
        'w_in_even': nrm(ks[5], (N_EVEN, D_MODEL, d_in_even), D_MODEL ** -0.5),
        'conv_w': nrm(ks[6], (N_EVEN, CONV_WIDTH, D_CONV), CONV_WIDTH ** -0.5),
        'conv_b': nrm(ks[7], (N_EVEN, D_CONV), 0.01),
        'conv_ln_g': gain(ks[8], (N_EVEN, D_CONV)),
        'conv_ln_b': nrm(ks[9], (N_EVEN, D_CONV), 0.01),
        'w_out_even': nrm(ks[10], (N_EVEN, D_CONV + D_FOURIER, D_MODEL), (D_CONV + D_FOURIER) ** -0.5),
        'w_pool': nrm(ks[11], (N_ODD, POOL_GROUPS, POOL_GROUP_DIM, POOL_GROUP_DIM), POOL_GROUP_DIM ** -0.5),
        'pool_scale': 0.5 + 0.1 * jax.random.normal(ks[12], (N_ODD, D_MODEL), f32),
        'norm_xa': gain(ks[13], (DEPTH, D_MODEL)),
        'norm_mem': gain(ks[14], (DEPTH, D_MODEL)),
        'xa_wq': nrm(ks[15], (DEPTH, D_MODEL, D_MODEL), D_MODEL ** -0.5),
        'xa_wkv': nrm(ks[16], (DEPTH, D_MODEL, 2 * D_MODEL), D_MODEL ** -0.5),
        'xa_wo': nrm(ks[17], (DEPTH, D_MODEL, D_MODEL), D_MODEL ** -0.5),
        'norm_ffn': gain(ks[18], (DEPTH, D_MODEL)),
        'ffn_w_gate_up': nrm(ks[19], (DEPTH, D_MODEL, 2 * D_FF), D_MODEL ** -0.5),
        'ffn_w_down': nrm(ks[20], (DEPTH, D_FF, D_MODEL), D_FF ** -0.5),
        'norm_final': gain(ks[21], (D_MODEL,)),
    }


def reference(x_prompt, x_sample, mem_prompt, mem_sample, norm_mix, w_in_even, conv_w, conv_b,
              conv_ln_g, conv_ln_b, w_out_even, w_pool, pool_scale, norm_xa, norm_mem, xa_wq,
              xa_wkv, xa_wo, norm_ffn, ffn_w_gate_up, ffn_w_down, norm_final):
    y_prompt = trunk(x_prompt, mem_prompt, norm_mix, w_in_even, conv_w, conv_b, conv_ln_g,
                     conv_ln_b, w_out_even, w_pool, pool_scale, norm_xa, norm_mem, xa_wq,
                     xa_wkv, xa_wo, norm_ffn, ffn_w_gate_up, ffn_w_down, norm_final)
    y_sample = trunk(x_sample, mem_sample, norm_mix, w_in_even, conv_w, conv_b, conv_ln_g,
                     conv_ln_b, w_out_even, w_pool, pool_scale, norm_xa, norm_mem, xa_wq,
                     xa_wkv, xa_wo, norm_ffn, ffn_w_gate_up, ffn_w_down, norm_final)
    return (y_prompt, y_sample)
```

```python
import functools

import jax
import jax.numpy as jnp
import numpy as np
from jax import lax
from jax.experimental import pallas as pl
from jax.experimental.pallas import tpu as pltpu

D_MODEL = 1024
XA_HEADS = 4
XA_HEAD_DIM = D_MODEL // XA_HEADS
CONV_WIDTH = 31
CONV_HALF = CONV_WIDTH // 2
D_CONV = D_MODEL // 2
D_FOURIER = D_MODEL - D_CONV
FOURIER_GROUPS = 4
FOURIER_GROUP_DIM = D_FOURIER // FOURIER_GROUPS
POOL_WINDOWS = (2, 4, 8, 16)
POOL_GROUP_DIM = D_MODEL // len(POOL_WINDOWS)
RMS_EPS = 1e-6
LN_EPS = 1e-5

SUBLANES = 8
LANES = 128
BF16_ROWS = 16
CONV_HALO = 16
POOL_HALO = 8
CONV_ROW_CHUNK = 64
VMEM_LIMIT_BYTES = 56 * 1024 * 1024

BF16 = jnp.bfloat16
F32 = jnp.float32


def _seq_tile(seq_len, want):
    tile = min(want, seq_len)
    assert seq_len % tile == 0 and tile % CONV_ROW_CHUNK == 0
    return tile


def _params(n_axes):
    return pltpu.CompilerParams(dimension_semantics=("parallel",) * n_axes,
                                vmem_limit_bytes=VMEM_LIMIT_BYTES)


def _resident(shape):
    zeros = (0,) * len(shape)
    return pl.BlockSpec(shape, lambda *_: zeros, pipeline_mode=pl.Buffered(1))


def _rms(xf, g):
    return xf * lax.rsqrt(jnp.mean(xf * xf, axis=-1, keepdims=True) + RMS_EPS) * g


def _dot(a, b):
    return jnp.dot(a, b, preferred_element_type=F32)


def _even_in_kernel(x_ref, g_ref, w_in_ref, cdft_ref, hglu_ref, fa_ref, fb_ref):
    h = _rms(x_ref[0], g_ref[...]).astype(BF16)
    u = _dot(h, w_in_ref[...])
    val, gate = u[:, :D_CONV], u[:, D_CONV:2 * D_CONV]
    hglu_ref[0] = (val * jax.nn.sigmoid(gate)).astype(BF16)
    uf = u[:, 2 * D_CONV:].astype(BF16)
    fa, fb = [], []
    for grp in range(FOURIER_GROUPS):
        c0 = grp * FOURIER_GROUP_DIM
        ab = _dot(uf[:, c0:c0 + FOURIER_GROUP_DIM], cdft_ref[...])
        fa.append(ab[:, :FOURIER_GROUP_DIM])
        fb.append(ab[:, FOURIER_GROUP_DIM:])
    fa_ref[0] = jnp.concatenate(fa, axis=-1).astype(BF16)
    fb_ref[0] = jnp.concatenate(fb, axis=-1).astype(BF16)


def _even_in(x, g, w_in, cdft, tm):
    b, s, d = x.shape
    d_in = w_in.shape[1]
    tile = lambda width: pl.BlockSpec((1, tm, width), lambda i, j: (i, j, 0))
    out = jax.ShapeDtypeStruct((b, s, D_CONV), BF16)
    return pl.pallas_call(
        _even_in_kernel,
        grid=(b, s // tm),
        in_specs=[tile(d), _resident((1, d)), _resident((d, d_in)),
                  _resident((FOURIER_GROUP_DIM, 2 * FOURIER_GROUP_DIM))],
        out_specs=[tile(D_CONV), tile(D_FOURIER), tile(D_FOURIER)],
        out_shape=[out, out, out],
        compiler_params=_params(2),
        name="even_in",
    )(x, g, w_in, cdft)


def _seq_dft_kernel(m_ref, fa_ref, fb_ref, f_ref):
    s = fa_ref.shape[1]
    f = _dot(m_ref[:, :s], fa_ref[0]) + _dot(m_ref[:, s:], fb_ref[0])
    f_ref[0] = f.astype(BF16)


def _seq_dft(m, fa, fb, tk):
    b, s, c = fa.shape
    full = pl.BlockSpec((1, s, c), lambda i, j: (j, 0, 0))
    return pl.pallas_call(
        _seq_dft_kernel,
        grid=(s // tk, b),
        in_specs=[pl.BlockSpec((tk, 2 * s), lambda i, j: (i, 0)), full, full],
        out_specs=pl.BlockSpec((1, tk, c), lambda i, j: (j, i, 0)),
        out_shape=jax.ShapeDtypeStruct((b, s, c), BF16),
        compiler_params=_params(2),
        name="seq_dft",
    )(m, fa, fb)


def _even_out_kernel(x_ref, prev_ref, cur_ref, next_ref, f_ref, cw_ref, cb_ref, lg_ref, lb_ref,
                     w_out_ref, o_ref, win_ref, a_ref):
    t, nt = pl.program_id(1), pl.num_programs(1)
    tm = cur_ref.shape[1]
    win_ref[0:CONV_HALO, :] = jnp.where(t > 0, prev_ref[0].astype(F32), 0.0)
    win_ref[CONV_HALO:CONV_HALO + tm, :] = cur_ref[0].astype(F32)
    win_ref[CONV_HALO + tm:, :] = jnp.where(t < nt - 1, next_ref[0].astype(F32), 0.0)

    rc = CONV_ROW_CHUNK
    span = rc + 2 * CONV_HALO

    def chunk(c, carry):
        r0 = pl.multiple_of(c * rc, rc)
        accs = []
        for l0 in range(0, D_CONV, LANES):
            win = win_ref[pl.ds(r0, span), l0:l0 + LANES]
            acc = jnp.zeros((rc, LANES), F32)
            for r in range(SUBLANES):
                rolled = win if r == 0 else pltpu.roll(win, span - r, axis=0)
                for q in range(span // SUBLANES):
                    j = SUBLANES * q + r - 1
                    if 0 <= j < CONV_WIDTH:
                        acc = acc + cw_ref[j:j + 1, l0:l0 + LANES] * rolled[SUBLANES * q:SUBLANES * q + rc]
            accs.append(acc)
        conv = jnp.concatenate(accs, axis=-1) + cb_ref[...]
        mu = jnp.mean(conv, axis=-1, keepdims=True)
        xc = conv - mu
        y = xc * lax.rsqrt(jnp.mean(xc * xc, axis=-1, keepdims=True) + LN_EPS) * lg_ref[...] + lb_ref[...]
        a_ref[pl.ds(r0, rc), :] = (y * jax.nn.sigmoid(y)).astype(BF16)
        return carry

    lax.fori_loop(0, tm // rc, chunk, 0)
    cat = jnp.concatenate([a_ref[...], f_ref[0]], axis=-1)
    o_ref[0] = x_ref[0] + _dot(cat, w_out_ref[...])


def _even_out(x, hglu, f, conv_w, conv_b, ln_g, ln_b, w_out, tm):
    b, s, d = x.shape
    per_halo = tm // CONV_HALO
    last_halo = s // CONV_HALO - 1
    tile = lambda width: pl.BlockSpec((1, tm, width), lambda i, j: (i, j, 0))
    prev = pl.BlockSpec((1, CONV_HALO, D_CONV), lambda i, j: (i, jnp.maximum(j * per_halo - 1, 0), 0))
    nxt = pl.BlockSpec((1, CONV_HALO, D_CONV),
                       lambda i, j: (i, jnp.minimum((j + 1) * per_halo, last_halo), 0))
    return pl.pallas_call(
        _even_out_kernel,
        grid=(b, s // tm),
        in_specs=[tile(d), prev, tile(D_CONV), nxt, tile(D_FOURIER),
                  _resident(conv_w.shape), _resident((1, D_CONV)), _resident((1, D_CONV)),
                  _resident((1, D_CONV)), _resident((d, d))],
        out_specs=tile(d),
        out_shape=jax.ShapeDtypeStruct((b, s, d), F32),
        scratch_shapes=[pltpu.VMEM((tm + 2 * CONV_HALO, D_CONV), F32), pltpu.VMEM((tm, D_CONV), BF16)],
        compiler_params=_params(2),
        name="even_out",
    )(x, hglu, hglu, hglu, f, conv_w, conv_b, ln_g, ln_b, w_out)


def _pool_kernel(prev_ref, x_ref, next_ref, g_ref, w_ref, scale_ref, o_ref, *, seq_len):
    t, nt = pl.program_id(1), pl.num_programs(1)
    tm = x_ref.shape[1]
    n = tm + 2 * POOL_HALO
    x = x_ref[0]
    prev = jnp.where(t > 0, prev_ref[0], 0.0)
    nxt = jnp.where(t < nt - 1, next_ref[0], 0.0)
    hwin = _rms(jnp.concatenate([prev, x, nxt], axis=0), g_ref[...])
    pos = t * tm + lax.broadcasted_iota(jnp.int32, (tm, 1), 0)

    def ahead(v, k):
        return pltpu.roll(v, n - k, axis=0)

    def rows_from(v, k):
        return v[k:k + tm] if k % SUBLANES == 0 else ahead(v, k)[:tm]

    ys = []
    for grp, w in enumerate(POOL_WINDOWS):
        assert w <= 2 * POOL_HALO
        c0 = grp * POOL_GROUP_DIM
        hg = hwin[:, c0:c0 + POOL_GROUP_DIM]
        acc, width = hg, 1
        while 2 * width <= min(w, POOL_HALO):
            acc = acc + ahead(acc, width)
            width *= 2
        tot = rows_from(acc, POOL_HALO - w // 2)
        for k in range(1, w // width):
            tot = tot + rows_from(acc, POOL_HALO - w // 2 + k * width)
        cnt = jnp.minimum(pos + w // 2, seq_len) - jnp.maximum(pos - w // 2, 0)
        p = tot / cnt.astype(F32) - hg[POOL_HALO:POOL_HALO + tm]
        ys.append(_dot(p.astype(BF16), w_ref[grp]))
    o_ref[0] = x + jnp.concatenate(ys, axis=-1) * scale_ref[...]


def _pool(x, g, w_pool, scale, tm):
    b, s, d = x.shape
    per_halo = tm // POOL_HALO
    last_halo = s // POOL_HALO - 1
    tile = pl.BlockSpec((1, tm, d), lambda i, j: (i, j, 0))
    prev = pl.BlockSpec((1, POOL_HALO, d), lambda i, j: (i, jnp.maximum(j * per_halo - 1, 0), 0))
    nxt = pl.BlockSpec((1, POOL_HALO, d), lambda i, j: (i, jnp.minimum((j + 1) * per_halo, last_halo), 0))
    return pl.pallas_call(
        functools.partial(_pool_kernel, seq_len=s),
        grid=(b, s // tm),
        in_specs=[prev, tile, nxt, _resident((1, d)), _resident(w_pool.shape), _resident((1, d))],
        out_specs=tile,
        out_shape=jax.ShapeDtypeStruct((b, s, d), F32),
        compiler_params=_params(2),
        name="pool",
    )(x, x, x, g, w_pool, scale)


def _kv_kernel(mem_ref, g_ref, wkv_ref, k_ref, v_ref):
    m = _rms(mem_ref[0], g_ref[...]).astype(BF16)
    kv = _dot(m, wkv_ref[...])
    k_ref[0] = kv[:, :D_MODEL].astype(BF16)
    v_ref[0] = kv[:, D_MODEL:].astype(BF16)


def _kv(mem, g, wkv):
    b, m, d = mem.shape
    blk = pl.BlockSpec((1, m, d), lambda i: (i, 0, 0))
    out = jax.ShapeDtypeStruct((b, m, d), BF16)
    return pl.pallas_call(
        _kv_kernel,
        grid=(b,),
        in_specs=[blk, _resident((1, d)), _resident((d, 2 * d))],
        out_specs=[blk, blk],
        out_shape=[out, out],
        compiler_params=_params(1),
        name="mem_kv",
    )(mem, g, wkv)


def _attn_kernel(x_ref, g_ref, wq_ref, k_ref, v_ref, wo_ref, o_ref):
    x = x_ref[0]
    h = _rms(x, g_ref[...]).astype(BF16)
    q = (_dot(h, wq_ref[...]) * (XA_HEAD_DIM ** -0.5)).astype(BF16)
    heads = []
    for hd in range(XA_HEADS):
        c0 = hd * XA_HEAD_DIM
        sc = lax.dot_general(q[:, c0:c0 + XA_HEAD_DIM], k_ref[0, :, c0:c0 + XA_HEAD_DIM],
                             (((1,), (1,)), ((), ())), preferred_element_type=F32)
        e = jnp.exp(sc - jnp.max(sc, axis=-1, keepdims=True))
        p = (e / jnp.sum(e, axis=-1, keepdims=True)).astype(BF16)
        heads.append(_dot(p, v_ref[0, :, c0:c0 + XA_HEAD_DIM]).astype(BF16))
    o_ref[0] = x + _dot(jnp.concatenate(heads, axis=-1), wo_ref[...])


def _attn(x, g, wq, k, v, wo, tm):
    b, s, d = x.shape
    m = k.shape[1]
    tile = pl.BlockSpec((1, tm, d), lambda i, j: (i, j, 0))
    memblk = pl.BlockSpec((1, m, d), lambda i, j: (i, 0, 0))
    return pl.pallas_call(
        _attn_kernel,
        grid=(b, s // tm),
        in_specs=[tile, _resident((1, d)), _resident((d, d)), memblk, memblk, _resident((d, d))],
        out_specs=tile,
        out_shape=jax.ShapeDtypeStruct((b, s, d), F32),
        compiler_params=_params(2),
        name="cross_attn",
    )(x, g, wq, k, v, wo)


def _ffn_chunks(d_ff):
    mxu = 256
    assert d_ff % mxu == 0
    n = d_ff // mxu
    sizes = [n // 2, n - n // 2] if n > 4 else [n]
    out, c0 = [], 0
    for sz in sizes:
        out.append((c0, sz * mxu))
        c0 += sz * mxu
    return out


def _ffn_kernel(x_ref, g_ref, wgu_ref, wd_ref, gf_ref, o_ref, *, final_norm):
    d_ff = wd_ref.shape[0]
    x = x_ref[...]
    h = _rms(x, g_ref[...]).astype(BF16)
    acc = x
    for c0, width in _ffn_chunks(d_ff):
        gate = _dot(h, wgu_ref[:, c0:c0 + width])
        up = _dot(h, wgu_ref[:, d_ff + c0:d_ff + c0 + width])
        act = (gate * jax.nn.sigmoid(gate) * up).astype(BF16)
        acc = acc + _dot(act, wd_ref[c0:c0 + width, :])
    o_ref[...] = _rms(acc, gf_ref[...]) if final_norm else acc


def _ffn(x, g, wgu, wd, g_final, tm, final_norm):
    n, d = x.shape
    d_ff = wd.shape[0]
    tile = pl.BlockSpec((tm, d), lambda i: (i, 0))
    return pl.pallas_call(
        functools.partial(_ffn_kernel, final_norm=final_norm),
        grid=(n // tm,),
        in_specs=[tile, _resident((1, d)), _resident((d, 2 * d_ff)), _resident((d_ff, d)),
                  _resident((1, d))],
        out_specs=tile,
        out_shape=jax.ShapeDtypeStruct((n, d), F32),
        compiler_params=_params(1),
        name="ffn",
    )(x, g, wgu, wd, g_final)


def _channel_dft():
    n = FOURIER_GROUP_DIM
    idx = (np.arange(n)[:, None] * np.arange(n)[None, :]) % n
    ang = 2.0 * np.pi * idx / n
    m = np.concatenate([np.cos(ang), np.sin(ang)], axis=1) / np.sqrt(n)
    return jnp.asarray(m, dtype=F32).astype(BF16)


def _sequence_dft(s):
    k = lax.broadcasted_iota(jnp.int32, (s, s), 0)
    n = lax.broadcasted_iota(jnp.int32, (s, s), 1)
    ang = ((k * n) % s).astype(F32) * (2.0 * np.pi / s)
    m = jnp.concatenate([jnp.cos(ang), -jnp.sin(ang)], axis=1) * (s ** -0.5)
    return m.astype(BF16)


def _trunk(x, mem, p, tm=512, tk=512, tm_ffn=512):
    b, s, d = x.shape
    depth = p["norm_mix"].shape[0]
    tm = _seq_tile(s, tm)
    tk = _seq_tile(s, tk)
    row = lambda v: v.reshape(1, -1)
    for l in range(depth):
        if l % 2 == 0:
            e = l // 2
            hglu, fa, fb = _even_in(x, row(p["norm_mix"][l]), p["w_in_even"][e], p["cdft"], tm)
            f = _seq_dft(_sequence_dft(s), fa, fb, tk)
            x = _even_out(x, hglu, f, p["conv_w"][e], row(p["conv_b"][e]), row(p["conv_ln_g"][e]),
                          row(p["conv_ln_b"][e]), p["w_out_even"][e], tm)
        else:
            o = l // 2
            x = _pool(x, row(p["norm_mix"][l]), p["w_pool"][o], row(p["pool_scale"][o]), tm)
        k, v = _kv(mem, row(p["norm_mem"][l]), p["xa_wkv"][l])
        x = _attn(x, row(p["norm_xa"][l]), p["xa_wq"][l], k, v, p["xa_wo"][l], tm)
        x = _ffn(x.reshape(b * s, d), row(p["norm_ffn"][l]), p["ffn_w_gate_up"][l],
                 p["ffn_w_down"][l], row(p["norm_final"]), _seq_tile(b * s, tm_ffn),
                 final_norm=(l == depth - 1)).reshape(b, s, d)
    return x


def _prepare(norm_mix, w_in_even, conv_w, conv_b, conv_ln_g, conv_ln_b, w_out_even, w_pool, pool_scale,
             norm_xa, norm_mem, xa_wq, xa_wkv, xa_wo, norm_ffn, ffn_w_gate_up, ffn_w_down, norm_final):
    conv_w = jnp.pad(conv_w, ((0, 0), (0, -conv_w.shape[1] % SUBLANES), (0, 0)))
    return dict(
        norm_mix=norm_mix, w_in_even=w_in_even.astype(BF16), conv_w=conv_w, conv_b=conv_b,
        conv_ln_g=conv_ln_g, conv_ln_b=conv_ln_b, w_out_even=w_out_even.astype(BF16),
        w_pool=w_pool.astype(BF16), pool_scale=pool_scale, norm_xa=norm_xa, norm_mem=norm_mem,
        xa_wq=xa_wq.astype(BF16), xa_wkv=xa_wkv.astype(BF16), xa_wo=xa_wo.astype(BF16),
        norm_ffn=norm_ffn, ffn_w_gate_up=ffn_w_gate_up.astype(BF16),
        ffn_w_down=ffn_w_down.astype(BF16), norm_final=norm_final, cdft=_channel_dft())


def kernel(x_prompt, x_sample, mem_prompt, mem_sample, norm_mix, w_in_even, conv_w, conv_b, conv_ln_g,
           conv_ln_b, w_out_even, w_pool, pool_scale, norm_xa, norm_mem, xa_wq, xa_wkv, xa_wo, norm_ffn,
           ffn_w_gate_up, ffn_w_down, norm_final):
    p = _prepare(norm_mix, w_in_even, conv_w, conv_b, conv_ln_g, conv_ln_b, w_out_even, w_pool,
                 pool_scale, norm_xa, norm_mem, xa_wq, xa_wkv, xa_wo, norm_ffn, ffn_w_gate_up,
                 ffn_w_down, norm_final)
    return _trunk(x_prompt, mem_prompt, p), _trunk(x_sample, mem_sample, p)
```

```python
import functools

import jax
import jax.numpy as jnp
import numpy as np
from jax import lax
from jax.experimental import pallas as pl
from jax.experimental.pallas import tpu as pltpu

D_MODEL = 1024
XA_HEADS = 4
XA_HEAD_DIM = D_MODEL // XA_HEADS
CONV_WIDTH = 31
CONV_HALF = CONV_WIDTH // 2
D_CONV = D_MODEL // 2
D_FOURIER = D_MODEL - D_CONV
FOURIER_GROUPS = 4
FOURIER_GROUP_DIM = D_FOURIER // FOURIER_GROUPS
POOL_WINDOWS = (2, 4, 8, 16)
POOL_GROUP_DIM = D_MODEL // len(POOL_WINDOWS)
RMS_EPS = 1e-6
LN_EPS = 1e-5

SUBLANES = 8
LANES = 128
BF16_ROWS = 16
CONV_HALO = 16
POOL_HALO = 8
CONV_ROW_CHUNK = 64
VMEM_LIMIT_BYTES = 56 * 1024 * 1024
DFT_TABLE_BYTES = 8 * 1024 * 1024
DFT_GEN_COLS = 512

BF16 = jnp.bfloat16
F32 = jnp.float32


def _seq_tile(seq_len, want):
    tile = min(want, seq_len)
    assert seq_len % tile == 0 and tile % CONV_ROW_CHUNK == 0
    return tile


def _params(n_axes):
    return pltpu.CompilerParams(dimension_semantics=("parallel",) * n_axes,
                                vmem_limit_bytes=VMEM_LIMIT_BYTES)


def _resident(shape):
    zeros = (0,) * len(shape)
    return pl.BlockSpec(shape, lambda *_: zeros, pipeline_mode=pl.Buffered(1))


def _rms(xf, g):
    return xf * lax.rsqrt(jnp.mean(xf * xf, axis=-1, keepdims=True) + RMS_EPS) * g


def _dot(a, b):
    return jnp.dot(a, b, preferred_element_type=F32)


def _even_in_kernel(x_ref, g_ref, w_in_ref, cdft_ref, hglu_ref, fa_ref, fb_ref):
    h = _rms(x_ref[0], g_ref[...]).astype(BF16)
    u = _dot(h, w_in_ref[...])
    val, gate = u[:, :D_CONV], u[:, D_CONV:2 * D_CONV]
    hglu_ref[0] = (val * jax.nn.sigmoid(gate)).astype(BF16)
    uf = u[:, 2 * D_CONV:].astype(BF16)
    fa, fb = [], []
    for grp in range(FOURIER_GROUPS):
        c0 = grp * FOURIER_GROUP_DIM
        ab = _dot(uf[:, c0:c0 + FOURIER_GROUP_DIM], cdft_ref[...])
        fa.append(ab[:, :FOURIER_GROUP_DIM])
        fb.append(ab[:, FOURIER_GROUP_DIM:])
    fa_ref[0] = jnp.concatenate(fa, axis=-1).astype(BF16)
    fb_ref[0] = jnp.concatenate(fb, axis=-1).astype(BF16)


def _even_in(x, g, w_in, cdft, tm):
    b, s, d = x.shape
    d_in = w_in.shape[1]
    tile = lambda width: pl.BlockSpec((1, tm, width), lambda i, j: (i, j, 0))
    out = jax.ShapeDtypeStruct((b, s, D_CONV), BF16)
    return pl.pallas_call(
        _even_in_kernel,
        grid=(b, s // tm),
        in_specs=[tile(d), _resident((1, d)), _resident((d, d_in)),
                  _resident((FOURIER_GROUP_DIM, 2 * FOURIER_GROUP_DIM))],
        out_specs=[tile(D_CONV), tile(D_FOURIER), tile(D_FOURIER)],
        out_shape=[out, out, out],
        compiler_params=_params(2),
        name="even_in",
    )(x, g, w_in, cdft)


def _seq_dft_kernel(bc_ref, bs_ref, oc_ref, os_ref, fa_ref, fb_ref, f_ref, m_ref):
    s = fa_ref.shape[1]

    @pl.when(pl.program_id(1) == 0)
    def _():
        width = min(DFT_GEN_COLS, s)
        for c0 in range(0, s, width):
            cols = slice(c0, c0 + width)
            bc, bs = bc_ref[0, :, cols], bs_ref[0, :, cols]
            oc, osn = oc_ref[:, cols], os_ref[:, cols]
            m_ref[:, c0:c0 + width] = (bc * oc - bs * osn).astype(BF16)
            m_ref[:, s + c0:s + c0 + width] = (-(bs * oc + bc * osn)).astype(BF16)

    f = _dot(m_ref[:, :s], fa_ref[0]) + _dot(m_ref[:, s:], fb_ref[0])
    f_ref[0] = f.astype(BF16)


def _seq_dft(tables, fa, fb, tk):
    base_cos, base_sin, off_cos, off_sin = tables
    b, s, c = fa.shape
    full = pl.BlockSpec((1, s, c), lambda i, j: (j, 0, 0))
    base = pl.BlockSpec((1, 1, s), lambda i, j: (i, 0, 0))
    return pl.pallas_call(
        _seq_dft_kernel,
        grid=(s // tk, b),
        in_specs=[base, base, _resident((tk, s)), _resident((tk, s)), full, full],
        out_specs=pl.BlockSpec((1, tk, c), lambda i, j: (j, i, 0)),
        out_shape=jax.ShapeDtypeStruct((b, s, c), BF16),
        scratch_shapes=[pltpu.VMEM((tk, 2 * s), BF16)],
        compiler_params=pltpu.CompilerParams(dimension_semantics=("arbitrary", "arbitrary"),
                                             vmem_limit_bytes=VMEM_LIMIT_BYTES),
        name="seq_dft",
    )(base_cos, base_sin, off_cos, off_sin, fa, fb)


def _even_out_kernel(x_ref, prev_ref, cur_ref, next_ref, f_ref, cw_ref, cb_ref, lg_ref, lb_ref,
                     w_out_ref, o_ref, win_ref, conv_ref):
    t, nt = pl.program_id(1), pl.num_programs(1)
    tm = cur_ref.shape[1]
    prev = jnp.where(t > 0, prev_ref[0].astype(F32), 0.0)
    nxt = jnp.where(t < nt - 1, next_ref[0].astype(F32), 0.0)
    for blk in range(D_CONV // LANES):
        lanes = slice(blk * LANES, (blk + 1) * LANES)
        win_ref[blk, 0:CONV_HALO, :] = prev[:, lanes]
        win_ref[blk, CONV_HALO:CONV_HALO + tm, :] = cur_ref[0, :, lanes].astype(F32)
        win_ref[blk, CONV_HALO + tm:, :] = nxt[:, lanes]

    rc = CONV_ROW_CHUNK
    first = CONV_HALO - CONV_HALF

    def chunk(c, carry):
        r0 = pl.multiple_of(c * rc, rc)
        for blk in range(D_CONV // LANES):
            lanes = slice(blk * LANES, (blk + 1) * LANES)
            acc = jnp.zeros((rc, LANES), F32)
            for j in range(CONV_WIDTH):
                acc = acc + cw_ref[j:j + 1, lanes] * win_ref[blk, pl.ds(r0 + first + j, rc), :]
            conv_ref[pl.ds(r0, rc), lanes] = acc
        return carry

    lax.fori_loop(0, tm // rc, chunk, 0)
    conv = conv_ref[...] + cb_ref[...]
    mu = jnp.mean(conv, axis=-1, keepdims=True)
    xc = conv - mu
    y = xc * lax.rsqrt(jnp.mean(xc * xc, axis=-1, keepdims=True) + LN_EPS) * lg_ref[...] + lb_ref[...]
    a = (y * jax.nn.sigmoid(y)).astype(BF16)
    cat = jnp.concatenate([a, f_ref[0]], axis=-1)
    o_ref[0] = x_ref[0] + _dot(cat, w_out_ref[...])


def _even_out(x, hglu, f, conv_w, conv_b, ln_g, ln_b, w_out, tm):
    b, s, d = x.shape
    per_halo = tm // CONV_HALO
    last_halo = s // CONV_HALO - 1
    tile = lambda width: pl.BlockSpec((1, tm, width), lambda i, j: (i, j, 0))
    prev = pl.BlockSpec((1, CONV_HALO, D_CONV), lambda i, j: (i, jnp.maximum(j * per_halo - 1, 0), 0))
    nxt = pl.BlockSpec((1, CONV_HALO, D_CONV),
                       lambda i, j: (i, jnp.minimum((j + 1) * per_halo, last_halo), 0))
    return pl.pallas_call(
        _even_out_kernel,
        grid=(b, s // tm),
        in_specs=[tile(d), prev, tile(D_CONV), nxt, tile(D_FOURIER),
                  _resident(conv_w.shape), _resident((1, D_CONV)), _resident((1, D_CONV)),
                  _resident((1, D_CONV)), _resident((d, d))],
        out_specs=tile(d),
        out_shape=jax.ShapeDtypeStruct((b, s, d), F32),
        scratch_shapes=[pltpu.VMEM((D_CONV // LANES, tm + 2 * CONV_HALO, LANES), F32),
                        pltpu.VMEM((tm, D_CONV), F32)],
        compiler_params=_params(2),
        name="even_out",
    )(x, hglu, hglu, hglu, f, conv_w, conv_b, ln_g, ln_b, w_out)


def _pool_mix(prev, x, nxt, g, w_ref, scale, t, seq_len):
    tm = x.shape[0]
    n = tm + 2 * POOL_HALO
    hwin = _rms(jnp.concatenate([prev, x, nxt], axis=0), g)
    pos = t * tm + lax.broadcasted_iota(jnp.int32, (tm, 1), 0)

    def ahead(v, k):
        return pltpu.roll(v, n - k, axis=0)

    def rows_from(v, k):
        return v[k:k + tm] if k % SUBLANES == 0 else ahead(v, k)[:tm]

    ys = []
    for grp, w in enumerate(POOL_WINDOWS):
        assert w <= 2 * POOL_HALO
        c0 = grp * POOL_GROUP_DIM
        hg = hwin[:, c0:c0 + POOL_GROUP_DIM]
        acc, width = hg, 1
        while 2 * width <= min(w, POOL_HALO):
            acc = acc + ahead(acc, width)
            width *= 2
        tot = rows_from(acc, POOL_HALO - w // 2)
        for k in range(1, w // width):
            tot = tot + rows_from(acc, POOL_HALO - w // 2 + k * width)
        cnt = jnp.minimum(pos + w // 2, seq_len) - jnp.maximum(pos - w // 2, 0)
        p = tot / cnt.astype(F32) - hg[POOL_HALO:POOL_HALO + tm]
        ys.append(_dot(p.astype(BF16), w_ref[grp]))
    return x + jnp.concatenate(ys, axis=-1) * scale


SCORE_SCALE_LOG2 = float(XA_HEAD_DIM ** -0.5 * np.log2(np.e))


def _kv_kernel(mem_ref, g_ref, wkv_ref, k_ref, v_ref):
    m = _rms(mem_ref[0], g_ref[...]).astype(BF16)
    kv = _dot(m, wkv_ref[...])
    k_ref[0] = (kv[:, :D_MODEL] * SCORE_SCALE_LOG2).astype(BF16)
    v_ref[0] = kv[:, D_MODEL:].astype(BF16)


def _kv(mem, g, wkv):
    b, m, d = mem.shape
    blk = pl.BlockSpec((1, m, d), lambda i: (i, 0, 0))
    out = jax.ShapeDtypeStruct((b, m, d), BF16)
    return pl.pallas_call(
        _kv_kernel,
        grid=(b,),
        in_specs=[blk, _resident((1, d)), _resident((d, 2 * d))],
        out_specs=[blk, blk],
        out_shape=[out, out],
        compiler_params=_params(1),
        name="mem_kv",
    )(mem, g, wkv)


def _attend(x, g, wq_ref, k_ref, v_ref, wo_ref):
    h = _rms(x, g).astype(BF16)
    q = _dot(h, wq_ref[...]).astype(BF16)
    heads = []
    for hd in range(XA_HEADS):
        c0 = hd * XA_HEAD_DIM
        z = lax.dot_general(q[:, c0:c0 + XA_HEAD_DIM], k_ref[0, :, c0:c0 + XA_HEAD_DIM],
                            (((1,), (1,)), ((), ())), preferred_element_type=F32)
        e = jnp.exp2(z - jnp.max(z, axis=-1, keepdims=True))
        o = _dot(e.astype(BF16), v_ref[0, :, c0:c0 + XA_HEAD_DIM])
        heads.append((o / jnp.sum(e, axis=-1, keepdims=True)).astype(BF16))
    return x + _dot(jnp.concatenate(heads, axis=-1), wo_ref[...])


def _attn_kernel(x_ref, g_ref, wq_ref, k_ref, v_ref, wo_ref, o_ref):
    o_ref[0] = _attend(x_ref[0], g_ref[...], wq_ref, k_ref, v_ref, wo_ref)


def _pool_attn_kernel(prev_ref, x_ref, next_ref, gp_ref, wp_ref, scale_ref, g_ref, wq_ref, k_ref, v_ref,
                      wo_ref, o_ref, *, seq_len):
    t, nt = pl.program_id(1), pl.num_programs(1)
    prev = jnp.where(t > 0, prev_ref[0], 0.0)
    nxt = jnp.where(t < nt - 1, next_ref[0], 0.0)
    x = _pool_mix(prev, x_ref[0], nxt, gp_ref[...], wp_ref, scale_ref[...], t, seq_len)
    o_ref[0] = _attend(x, g_ref[...], wq_ref, k_ref, v_ref, wo_ref)


def _attn(x, g, wq, k, v, wo, tm, pool=None):
    b, s, d = x.shape
    m = k.shape[1]
    tile = pl.BlockSpec((1, tm, d), lambda i, j: (i, j, 0))
    memblk = pl.BlockSpec((1, m, d), lambda i, j: (i, 0, 0))
    attn_specs = [_resident((1, d)), _resident((d, d)), memblk, memblk, _resident((d, d))]
    attn_args = (g, wq, k, v, wo)
    if pool is None:
        body, specs, args = _attn_kernel, [tile], (x,)
    else:
        per_halo = tm // POOL_HALO
        last_halo = s // POOL_HALO - 1
        prev = pl.BlockSpec((1, POOL_HALO, d), lambda i, j: (i, jnp.maximum(j * per_halo - 1, 0), 0))
        nxt = pl.BlockSpec((1, POOL_HALO, d),
                           lambda i, j: (i, jnp.minimum((j + 1) * per_halo, last_halo), 0))
        gp, wp, scale = pool
        body = functools.partial(_pool_attn_kernel, seq_len=s)
        specs = [prev, tile, nxt, _resident((1, d)), _resident(wp.shape), _resident((1, d))]
        args = (x, x, x, gp, wp, scale)
    return pl.pallas_call(
        body,
        grid=(b, s // tm),
        in_specs=specs + attn_specs,
        out_specs=tile,
        out_shape=jax.ShapeDtypeStruct((b, s, d), F32),
        compiler_params=_params(2),
        name="cross_attn" if pool is None else "pool_cross_attn",
    )(*args, *attn_args)


def _ffn_chunks(d_ff):
    mxu = 256
    assert d_ff % mxu == 0
    n = d_ff // mxu
    sizes = [n // 2, n - n // 2] if n > 4 else [n]
    out, c0 = [], 0
    for sz in sizes:
        out.append((c0, sz * mxu))
        c0 += sz * mxu
    return out


def _ffn_kernel(x_ref, g_ref, wgu_ref, wd_ref, gf_ref, o_ref, *, final_norm):
    d_ff = wd_ref.shape[0]
    x = x_ref[...]
    h = _rms(x, g_ref[...]).astype(BF16)
    acc = x
    for c0, width in _ffn_chunks(d_ff):
        gate = _dot(h, wgu_ref[:, c0:c0 + width])
        up = _dot(h, wgu_ref[:, d_ff + c0:d_ff + c0 + width])
        act = (gate * jax.nn.sigmoid(gate) * up).astype(BF16)
        acc = acc + _dot(act, wd_ref[c0:c0 + width, :])
    o_ref[...] = _rms(acc, gf_ref[...]) if final_norm else acc


def _ffn(x, g, wgu, wd, g_final, tm, final_norm):
    n, d = x.shape
    d_ff = wd.shape[0]
    tile = pl.BlockSpec((tm, d), lambda i: (i, 0))
    return pl.pallas_call(
        functools.partial(_ffn_kernel, final_norm=final_norm),
        grid=(n // tm,),
        in_specs=[tile, _resident((1, d)), _resident((d, 2 * d_ff)), _resident((d_ff, d)),
                  _resident((1, d))],
        out_specs=tile,
        out_shape=jax.ShapeDtypeStruct((n, d), F32),
        compiler_params=_params(1),
        name="ffn",
    )(x, g, wgu, wd, g_final)


def _channel_dft():
    n = FOURIER_GROUP_DIM
    idx = (np.arange(n)[:, None] * np.arange(n)[None, :]) % n
    ang = 2.0 * np.pi * idx / n
    m = np.concatenate([np.cos(ang), np.sin(ang)], axis=1) / np.sqrt(n)
    return jnp.asarray(m, dtype=F32).astype(BF16)


def _sequence_dft_tables(s, tk):
    def cos_sin(k):
        n = lax.broadcasted_iota(jnp.int32, (k.shape[0], s), 1)
        ang = ((k[:, None] * n) % s).astype(F32) * (2.0 * np.pi / s)
        return jnp.cos(ang), jnp.sin(ang)

    base_cos, base_sin = cos_sin(jnp.arange(0, s, tk, dtype=jnp.int32))
    off_cos, off_sin = cos_sin(jnp.arange(tk, dtype=jnp.int32))
    scale = s ** -0.5
    return base_cos[:, None, :], base_sin[:, None, :], off_cos * scale, off_sin * scale


def _trunk(x, mem, p, tm=512, tk=None, tm_ffn=512):
    b, s, d = x.shape
    depth = p["norm_mix"].shape[0]
    tm = _seq_tile(s, tm)
    tk = _seq_tile(s, tk or DFT_TABLE_BYTES // (2 * 4 * s))
    row = lambda v: v.reshape(1, -1)
    for l in range(depth):
        if l % 2 == 0:
            e = l // 2
            hglu, fa, fb = _even_in(x, row(p["norm_mix"][l]), p["w_in_even"][e], p["cdft"], tm)
            f = _seq_dft(_sequence_dft_tables(s, tk), fa, fb, tk)
            x = _even_out(x, hglu, f, p["conv_w"][e], row(p["conv_b"][e]), row(p["conv_ln_g"][e]),
                          row(p["conv_ln_b"][e]), p["w_out_even"][e], tm)
            pool = None
        else:
            o = l // 2
            pool = (row(p["norm_mix"][l]), p["w_pool"][o], row(p["pool_scale"][o]))
        k, v = _kv(mem, row(p["norm_mem"][l]), p["xa_wkv"][l])
        x = _attn(x, row(p["norm_xa"][l]), p["xa_wq"][l], k, v, p["xa_wo"][l], tm, pool)
        x = _ffn(x.reshape(b * s, d), row(p["norm_ffn"][l]), p["ffn_w_gate_up"][l],
                 p["ffn_w_down"][l], row(p["norm_final"]), _seq_tile(b * s, tm_ffn),
                 final_norm=(l == depth - 1)).reshape(b, s, d)
    return x


def _prepare(norm_mix, w_in_even, conv_w, conv_b, conv_ln_g, conv_ln_b, w_out_even, w_pool, pool_scale,
             norm_xa, norm_mem, xa_wq, xa_wkv, xa_wo, norm_ffn, ffn_w_gate_up, ffn_w_down, norm_final):
    conv_w = jnp.pad(conv_w, ((0, 0), (0, -conv_w.shape[1] % SUBLANES), (0, 0)))
    return dict(
        norm_mix=norm_mix, w_in_even=w_in_even.astype(BF16), conv_w=conv_w, conv_b=conv_b,
        conv_ln_g=conv_ln_g, conv_ln_b=conv_ln_b, w_out_even=w_out_even.astype(BF16),
        w_pool=w_pool.astype(BF16), pool_scale=pool_scale, norm_xa=norm_xa, norm_mem=norm_mem,
        xa_wq=xa_wq.astype(BF16), xa_wkv=xa_wkv.astype(BF16), xa_wo=xa_wo.astype(BF16),
        norm_ffn=norm_ffn, ffn_w_gate_up=ffn_w_gate_up.astype(BF16),
        ffn_w_down=ffn_w_down.astype(BF16), norm_final=norm_final, cdft=_channel_dft())


def kernel(x_prompt, x_sample, mem_prompt, mem_sample, norm_mix, w_in_even, conv_w, conv_b, conv_ln_g,
           conv_ln_b, w_out_even, w_pool, pool_scale, norm_xa, norm_mem, xa_wq, xa_wkv, xa_wo, norm_ffn,
           ffn_w_gate_up, ffn_w_down, norm_final):
    p = _prepare(norm_mix, w_in_even, conv_w, conv_b, conv_ln_g, conv_ln_b, w_out_even, w_pool,
                 pool_scale, norm_xa, norm_mem, xa_wq, xa_wkv, xa_wo, norm_ffn, ffn_w_gate_up,
                 ffn_w_down, norm_final)
    return _trunk(x_prompt, mem_prompt, p), _trunk(x_sample, mem_sample, p)
```

```python
import functools

import jax
import jax.numpy as jnp
import numpy as np
from jax import lax
from jax.experimental import pallas as pl
from jax.experimental.pallas import tpu as pltpu

D_MODEL = 1024
XA_HEADS = 4
XA_HEAD_DIM = D_MODEL // XA_HEADS
CONV_WIDTH = 31
CONV_HALF = CONV_WIDTH // 2
D_CONV = D_MODEL // 2
D_FOURIER = D_MODEL - D_CONV
FOURIER_GROUPS = 4
FOURIER_GROUP_DIM = D_FOURIER // FOURIER_GROUPS
POOL_WINDOWS = (2, 4, 8, 16)
POOL_GROUP_DIM = D_MODEL // len(POOL_WINDOWS)
RMS_EPS = 1e-6
LN_EPS = 1e-5

SUBLANES = 8
LANES = 128
BF16_ROWS = 16
CONV_HALO = 16
POOL_HALO = 8
CONV_ROW_CHUNK = 64
VMEM_LIMIT_BYTES = 56 * 1024 * 1024
DFT_RADIX = 4
DFT_ROW_CHUNK = 32
DFT_ROW_TILE = 512

BF16 = jnp.bfloat16
F32 = jnp.float32


def _seq_tile(seq_len, want):
    tile = min(want, seq_len)
    assert seq_len % tile == 0 and tile % CONV_ROW_CHUNK == 0
    return tile


def _params(n_axes):
    return pltpu.CompilerParams(dimension_semantics=("parallel",) * n_axes,
                                vmem_limit_bytes=VMEM_LIMIT_BYTES)


def _resident(shape):
    zeros = (0,) * len(shape)
    return pl.BlockSpec(shape, lambda *_: zeros, pipeline_mode=pl.Buffered(1))


def _rms(xf, g):
    return xf * lax.rsqrt(jnp.mean(xf * xf, axis=-1, keepdims=True) + RMS_EPS) * g


def _dot(a, b):
    return jnp.dot(a, b, preferred_element_type=F32)


def _even_in_kernel(x_ref, g_ref, w_in_ref, cdft_ref, hglu_ref, fa_ref, fb_ref):
    h = _rms(x_ref[0], g_ref[...]).astype(BF16)
    u = _dot(h, w_in_ref[...])
    val, gate = u[:, :D_CONV], u[:, D_CONV:2 * D_CONV]
    hglu_ref[0] = (val * jax.nn.sigmoid(gate)).astype(BF16)
    uf = u[:, 2 * D_CONV:].astype(BF16)
    fa, fb = [], []
    for grp in range(FOURIER_GROUPS):
        c0 = grp * FOURIER_GROUP_DIM
        ab = _dot(uf[:, c0:c0 + FOURIER_GROUP_DIM], cdft_ref[...])
        fa.append(ab[:, :FOURIER_GROUP_DIM])
        fb.append(ab[:, FOURIER_GROUP_DIM:])
    fa_ref[0] = jnp.concatenate(fa, axis=-1).astype(BF16)
    fb_ref[0] = jnp.concatenate(fb, axis=-1).astype(BF16)


def _even_in(x, g, w_in, cdft, tm):
    b, s, d = x.shape
    d_in = w_in.shape[1]
    tile = lambda width: pl.BlockSpec((1, tm, width), lambda i, j: (i, j, 0))
    out = jax.ShapeDtypeStruct((b, s, D_CONV), BF16)
    return pl.pallas_call(
        _even_in_kernel,
        grid=(b, s // tm),
        in_specs=[tile(d), _resident((1, d)), _resident((d, d_in)),
                  _resident((FOURIER_GROUP_DIM, 2 * FOURIER_GROUP_DIM))],
        out_specs=[tile(D_CONV), tile(D_FOURIER), tile(D_FOURIER)],
        out_shape=[out, out, out],
        compiler_params=_params(2),
        name="even_in",
    )(x, g, w_in, cdft)


def _seq_dft_kernel(a_ref, b_ref, tw_ref, m_ref, f_ref, z_ref):
    s, c = a_ref.shape[1], a_ref.shape[2]
    q = s // DFT_RADIX
    tk = f_ref.shape[1]
    rb = DFT_ROW_CHUNK

    @pl.when(pl.program_id(1) == 0)
    def _():
        def chunk(i, carry):
            r0 = pl.multiple_of(i * rb, rb)
            for l0 in range(0, c, LANES):
                lanes = slice(l0, l0 + LANES)
                a = [a_ref[0, pl.ds(n1 * q + r0, rb), lanes].astype(F32) for n1 in range(DFT_RADIX)]
                b = [b_ref[0, pl.ds(n1 * q + r0, rb), lanes].astype(F32) for n1 in range(DFT_RADIX)]
                sa02, da02, sa13, da13 = a[0] + a[2], a[0] - a[2], a[1] + a[3], a[1] - a[3]
                sb02, db02, sb13, db13 = b[0] + b[2], b[0] - b[2], b[1] + b[3], b[1] - b[3]
                re = [sa02 + sa13, da02 - db13, sa02 - sa13, da02 + db13]
                neg = [sb02 + sb13, db02 + da13, sb02 - sb13, db02 - da13]
                for k1 in range(DFT_RADIX):
                    zr, zn = re[k1], neg[k1]
                    if k1:
                        cs = tw_ref[k1 - 1, pl.ds(r0, rb), :]
                        sn = tw_ref[DFT_RADIX - 1 + k1 - 1, pl.ds(r0, rb), :]
                        zr, zn = cs * re[k1] - sn * neg[k1], cs * neg[k1] + sn * re[k1]
                    z_ref[k1, pl.ds(r0, rb), lanes] = zr.astype(BF16)
                    z_ref[k1, pl.ds(q + r0, rb), lanes] = zn.astype(BF16)
            return carry

        lax.fori_loop(0, q // rb, chunk, 0)

    rows = m_ref[pl.ds(pl.multiple_of(pl.program_id(1) * tk, tk), tk), :]
    for k1 in range(DFT_RADIX):
        f_ref[0, :, k1 * c:(k1 + 1) * c] = _dot(rows, z_ref[k1]).astype(BF16)


def _seq_dft(tables, fa, fb):
    tw, m = tables
    b, s, c = fa.shape
    q = s // DFT_RADIX
    tk = min(q, DFT_ROW_TILE)
    full = pl.BlockSpec((1, s, c), lambda i, j: (i, 0, 0))
    out = pl.pallas_call(
        _seq_dft_kernel,
        grid=(b, q // tk),
        in_specs=[full, full, _resident(tw.shape), _resident(m.shape)],
        out_specs=pl.BlockSpec((1, tk, DFT_RADIX * c), lambda i, j: (i, j, 0)),
        out_shape=jax.ShapeDtypeStruct((b, q, DFT_RADIX * c), BF16),
        scratch_shapes=[pltpu.VMEM((DFT_RADIX, 2 * q, c), BF16)],
        compiler_params=pltpu.CompilerParams(dimension_semantics=("arbitrary", "arbitrary"),
                                             vmem_limit_bytes=VMEM_LIMIT_BYTES),
        name="seq_dft",
    )(fa, fb, tw, m)
    return out.reshape(b, s, c)


def _even_out_kernel(x_ref, prev_ref, cur_ref, next_ref, f_ref, cw_ref, cb_ref, lg_ref, lb_ref,
                     w_out_ref, o_ref, win_ref, conv_ref):
    t, nt = pl.program_id(1), pl.num_programs(1)
    tm = cur_ref.shape[1]
    prev = jnp.where(t > 0, prev_ref[0].astype(F32), 0.0)
    nxt = jnp.where(t < nt - 1, next_ref[0].astype(F32), 0.0)
    for blk in range(D_CONV // LANES):
        lanes = slice(blk * LANES, (blk + 1) * LANES)
        win_ref[blk, 0:CONV_HALO, :] = prev[:, lanes]
        win_ref[blk, CONV_HALO:CONV_HALO + tm, :] = cur_ref[0, :, lanes].astype(F32)
        win_ref[blk, CONV_HALO + tm:, :] = nxt[:, lanes]

    rc = CONV_ROW_CHUNK
    first = CONV_HALO - CONV_HALF

    def chunk(c, carry):
        r0 = pl.multiple_of(c * rc, rc)
        for blk in range(D_CONV // LANES):
            lanes = slice(blk * LANES, (blk + 1) * LANES)
            acc = jnp.zeros((rc, LANES), F32)
            for j in range(CONV_WIDTH):
                acc = acc + cw_ref[j:j + 1, lanes] * win_ref[blk, pl.ds(r0 + first + j, rc), :]
            conv_ref[pl.ds(r0, rc), lanes] = acc
        return carry

    lax.fori_loop(0, tm // rc, chunk, 0)
    conv = conv_ref[...] + cb_ref[...]
    mu = jnp.mean(conv, axis=-1, keepdims=True)
    xc = conv - mu
    y = xc * lax.rsqrt(jnp.mean(xc * xc, axis=-1, keepdims=True) + LN_EPS) * lg_ref[...] + lb_ref[...]
    a = (y * jax.nn.sigmoid(y)).astype(BF16)
    cat = jnp.concatenate([a, f_ref[0]], axis=-1)
    o_ref[0] = x_ref[0] + _dot(cat, w_out_ref[...])


def _even_out(x, hglu, f, conv_w, conv_b, ln_g, ln_b, w_out, tm):
    b, s, d = x.shape
    per_halo = tm // CONV_HALO
    last_halo = s // CONV_HALO - 1
    tile = lambda width: pl.BlockSpec((1, tm, width), lambda i, j: (i, j, 0))
    prev = pl.BlockSpec((1, CONV_HALO, D_CONV), lambda i, j: (i, jnp.maximum(j * per_halo - 1, 0), 0))
    nxt = pl.BlockSpec((1, CONV_HALO, D_CONV),
                       lambda i, j: (i, jnp.minimum((j + 1) * per_halo, last_halo), 0))
    return pl.pallas_call(
        _even_out_kernel,
        grid=(b, s // tm),
        in_specs=[tile(d), prev, tile(D_CONV), nxt, tile(D_FOURIER),
                  _resident(conv_w.shape), _resident((1, D_CONV)), _resident((1, D_CONV)),
                  _resident((1, D_CONV)), _resident((d, d))],
        out_specs=tile(d),
        out_shape=jax.ShapeDtypeStruct((b, s, d), F32),
        scratch_shapes=[pltpu.VMEM((D_CONV // LANES, tm + 2 * CONV_HALO, LANES), F32),
                        pltpu.VMEM((tm, D_CONV), F32)],
        compiler_params=_params(2),
        name="even_out",
    )(x, hglu, hglu, hglu, f, conv_w, conv_b, ln_g, ln_b, w_out)


def _pool_mix(prev, x, nxt, g, w_ref, scale, t, seq_len):
    tm = x.shape[0]
    n = tm + 2 * POOL_HALO
    hwin = _rms(jnp.concatenate([prev, x, nxt], axis=0), g)
    pos = t * tm + lax.broadcasted_iota(jnp.int32, (tm, 1), 0)

    def ahead(v, k):
        return pltpu.roll(v, n - k, axis=0)

    def rows_from(v, k):
        return v[k:k + tm] if k % SUBLANES == 0 else ahead(v, k)[:tm]

    ys = []
    for grp, w in enumerate(POOL_WINDOWS):
        assert w <= 2 * POOL_HALO
        c0 = grp * POOL_GROUP_DIM
        hg = hwin[:, c0:c0 + POOL_GROUP_DIM]
        acc, width = hg, 1
        while 2 * width <= min(w, POOL_HALO):
            acc = acc + ahead(acc, width)
            width *= 2
        tot = rows_from(acc, POOL_HALO - w // 2)
        for k in range(1, w // width):
            tot = tot + rows_from(acc, POOL_HALO - w // 2 + k * width)
        cnt = jnp.minimum(pos + w // 2, seq_len) - jnp.maximum(pos - w // 2, 0)
        p = tot / cnt.astype(F32) - hg[POOL_HALO:POOL_HALO + tm]
        ys.append(_dot(p.astype(BF16), w_ref[grp]))
    return x + jnp.concatenate(ys, axis=-1) * scale


SCORE_SCALE_LOG2 = float(XA_HEAD_DIM ** -0.5 * np.log2(np.e))


def _kv_kernel(mem_ref, g_ref, wkv_ref, k_ref, v_ref):
    m = _rms(mem_ref[0], g_ref[...]).astype(BF16)
    kv = _dot(m, wkv_ref[...])
    k_ref[0] = (kv[:, :D_MODEL] * SCORE_SCALE_LOG2).astype(BF16)
    v_ref[0] = kv[:, D_MODEL:].astype(BF16)


def _kv(mem, g, wkv):
    b, m, d = mem.shape
    blk = pl.BlockSpec((1, m, d), lambda i: (i, 0, 0))
    out = jax.ShapeDtypeStruct((b, m, d), BF16)
    return pl.pallas_call(
        _kv_kernel,
        grid=(b,),
        in_specs=[blk, _resident((1, d)), _resident((d, 2 * d))],
        out_specs=[blk, blk],
        out_shape=[out, out],
        compiler_params=_params(1),
        name="mem_kv",
    )(mem, g, wkv)


def _attend(x, g, wq_ref, k_ref, v_ref, wo_ref):
    h = _rms(x, g).astype(BF16)
    q = _dot(h, wq_ref[...]).astype(BF16)
    heads = []
    for hd in range(XA_HEADS):
        c0 = hd * XA_HEAD_DIM
        z = lax.dot_general(q[:, c0:c0 + XA_HEAD_DIM], k_ref[0, :, c0:c0 + XA_HEAD_DIM],
                            (((1,), (1,)), ((), ())), preferred_element_type=F32)
        e = jnp.exp2(z - jnp.max(z, axis=-1, keepdims=True))
        o = _dot(e.astype(BF16), v_ref[0, :, c0:c0 + XA_HEAD_DIM])
        heads.append((o / jnp.sum(e, axis=-1, keepdims=True)).astype(BF16))
    return x + _dot(jnp.concatenate(heads, axis=-1), wo_ref[...])


def _attn_kernel(x_ref, g_ref, wq_ref, k_ref, v_ref, wo_ref, o_ref):
    o_ref[0] = _attend(x_ref[0], g_ref[...], wq_ref, k_ref, v_ref, wo_ref)


def _pool_attn_kernel(prev_ref, x_ref, next_ref, gp_ref, wp_ref, scale_ref, g_ref, wq_ref, k_ref, v_ref,
                      wo_ref, o_ref, *, seq_len):
    t, nt = pl.program_id(1), pl.num_programs(1)
    prev = jnp.where(t > 0, prev_ref[0], 0.0)
    nxt = jnp.where(t < nt - 1, next_ref[0], 0.0)
    x = _pool_mix(prev, x_ref[0], nxt, gp_ref[...], wp_ref, scale_ref[...], t, seq_len)
    o_ref[0] = _attend(x, g_ref[...], wq_ref, k_ref, v_ref, wo_ref)


def _attn(x, g, wq, k, v, wo, tm, pool=None):
    b, s, d = x.shape
    m = k.shape[1]
    tile = pl.BlockSpec((1, tm, d), lambda i, j: (i, j, 0))
    memblk = pl.BlockSpec((1, m, d), lambda i, j: (i, 0, 0))
    attn_specs = [_resident((1, d)), _resident((d, d)), memblk, memblk, _resident((d, d))]
    attn_args = (g, wq, k, v, wo)
    if pool is None:
        body, specs, args = _attn_kernel, [tile], (x,)
    else:
        per_halo = tm // POOL_HALO
        last_halo = s // POOL_HALO - 1
        prev = pl.BlockSpec((1, POOL_HALO, d), lambda i, j: (i, jnp.maximum(j * per_halo - 1, 0), 0))
        nxt = pl.BlockSpec((1, POOL_HALO, d),
                           lambda i, j: (i, jnp.minimum((j + 1) * per_halo, last_halo), 0))
        gp, wp, scale = pool
        body = functools.partial(_pool_attn_kernel, seq_len=s)
        specs = [prev, tile, nxt, _resident((1, d)), _resident(wp.shape), _resident((1, d))]
        args = (x, x, x, gp, wp, scale)
    return pl.pallas_call(
        body,
        grid=(b, s // tm),
        in_specs=specs + attn_specs,
        out_specs=tile,
        out_shape=jax.ShapeDtypeStruct((b, s, d), F32),
        compiler_params=_params(2),
        name="cross_attn" if pool is None else "pool_cross_attn",
    )(*args, *attn_args)


def _ffn_chunks(d_ff):
    mxu = 256
    assert d_ff % mxu == 0
    n = d_ff // mxu
    sizes = [n // 2, n - n // 2] if n > 4 else [n]
    out, c0 = [], 0
    for sz in sizes:
        out.append((c0, sz * mxu))
        c0 += sz * mxu
    return out


def _ffn_kernel(x_ref, g_ref, wgu_ref, wd_ref, gf_ref, o_ref, *, final_norm):
    d_ff = wd_ref.shape[0]
    x = x_ref[...]
    h = _rms(x, g_ref[...]).astype(BF16)
    acc = x
    for c0, width in _ffn_chunks(d_ff):
        gate = _dot(h, wgu_ref[:, c0:c0 + width])
        up = _dot(h, wgu_ref[:, d_ff + c0:d_ff + c0 + width])
        act = (gate * jax.nn.sigmoid(gate) * up).astype(BF16)
        acc = acc + _dot(act, wd_ref[c0:c0 + width, :])
    o_ref[...] = _rms(acc, gf_ref[...]) if final_norm else acc


def _ffn(x, g, wgu, wd, g_final, tm, final_norm):
    n, d = x.shape
    d_ff = wd.shape[0]
    tile = pl.BlockSpec((tm, d), lambda i: (i, 0))
    return pl.pallas_call(
        functools.partial(_ffn_kernel, final_norm=final_norm),
        grid=(n // tm,),
        in_specs=[tile, _resident((1, d)), _resident((d, 2 * d_ff)), _resident((d_ff, d)),
                  _resident((1, d))],
        out_specs=tile,
        out_shape=jax.ShapeDtypeStruct((n, d), F32),
        compiler_params=_params(1),
        name="ffn",
    )(x, g, wgu, wd, g_final)


def _channel_dft():
    n = FOURIER_GROUP_DIM
    idx = (np.arange(n)[:, None] * np.arange(n)[None, :]) % n
    ang = 2.0 * np.pi * idx / n
    m = np.concatenate([np.cos(ang), np.sin(ang)], axis=1) / np.sqrt(n)
    return jnp.asarray(m, dtype=F32).astype(BF16)


def _sequence_dft_tables(s):
    q = s // DFT_RADIX
    n2 = jnp.arange(q, dtype=jnp.int32)
    k1 = jnp.arange(1, DFT_RADIX, dtype=jnp.int32)
    ang = ((k1[:, None] * n2[None, :]) % s).astype(F32) * (2.0 * np.pi / s)
    tw = jnp.concatenate([jnp.cos(ang), jnp.sin(ang)], axis=0)
    tw = jnp.broadcast_to(tw[:, :, None], (2 * (DFT_RADIX - 1), q, LANES))
    ang = ((n2[:, None] * n2[None, :]) % q).astype(F32) * (2.0 * np.pi / q)
    m = jnp.concatenate([jnp.cos(ang), -jnp.sin(ang)], axis=1) * (s ** -0.5)
    return tw, m.astype(BF16)


def _trunk(x, mem, p, tm=1024, tm_attn=1024, tm_ffn=1024):
    b, s, d = x.shape
    depth = p["norm_mix"].shape[0]
    tm = _seq_tile(s, tm)
    tm_attn = _seq_tile(s, tm_attn)
    row = lambda v: v.reshape(1, -1)
    for l in range(depth):
        if l % 2 == 0:
            e = l // 2
            hglu, fa, fb = _even_in(x, row(p["norm_mix"][l]), p["w_in_even"][e], p["cdft"], tm)
            f = _seq_dft(_sequence_dft_tables(s), fa, fb)
            x = _even_out(x, hglu, f, p["conv_w"][e], row(p["conv_b"][e]), row(p["conv_ln_g"][e]),
                          row(p["conv_ln_b"][e]), p["w_out_even"][e], tm)
            pool = None
        else:
            o = l // 2
            pool = (row(p["norm_mix"][l]), p["w_pool"][o], row(p["pool_scale"][o]))
        k, v = _kv(mem, row(p["norm_mem"][l]), p["xa_wkv"][l])
        x = _attn(x, row(p["norm_xa"][l]), p["xa_wq"][l], k, v, p["xa_wo"][l], tm_attn, pool)
        x = _ffn(x.reshape(b * s, d), row(p["norm_ffn"][l]), p["ffn_w_gate_up"][l],
                 p["ffn_w_down"][l], row(p["norm_final"]), _seq_tile(b * s, tm_ffn),
                 final_norm=(l == depth - 1)).reshape(b, s, d)
    return x


def _prepare(norm_mix, w_in_even, conv_w, conv_b, conv_ln_g, conv_ln_b, w_out_even, w_pool, pool_scale,
             norm_xa, norm_mem, xa_wq, xa_wkv, xa_wo, norm_ffn, ffn_w_gate_up, ffn_w_down, norm_final):
    conv_w = jnp.pad(conv_w, ((0, 0), (0, -conv_w.shape[1] % SUBLANES), (0, 0)))
    return dict(
        norm_mix=norm_mix, w_in_even=w_in_even.astype(BF16), conv_w=conv_w, conv_b=conv_b,
        conv_ln_g=conv_ln_g, conv_ln_b=conv_ln_b, w_out_even=w_out_even.astype(BF16),
        w_pool=w_pool.astype(BF16), pool_scale=pool_scale, norm_xa=norm_xa, norm_mem=norm_mem,
        xa_wq=xa_wq.astype(BF16), xa_wkv=xa_wkv.astype(BF16), xa_wo=xa_wo.astype(BF16),
        norm_ffn=norm_ffn, ffn_w_gate_up=ffn_w_gate_up.astype(BF16),
        ffn_w_down=ffn_w_down.astype(BF16), norm_final=norm_final, cdft=_channel_dft())


def kernel(x_prompt, x_sample, mem_prompt, mem_sample, norm_mix, w_in_even, conv_w, conv_b, conv_ln_g,
           conv_ln_b, w_out_even, w_pool, pool_scale, norm_xa, norm_mem, xa_wq, xa_wkv, xa_wo, norm_ffn,
           ffn_w_gate_up, ffn_w_down, norm_final):
    p = _prepare(norm_mix, w_in_even, conv_w, conv_b, conv_ln_g, conv_ln_b, w_out_even, w_pool,
                 pool_scale, norm_xa, norm_mem, xa_wq, xa_wkv, xa_wo, norm_ffn, ffn_w_gate_up,
                 ffn_w_down, norm_final)
    return _trunk(x_prompt, mem_prompt, p), _trunk(x_sample, mem_sample, p)
```

```python
import functools

import jax
import jax.numpy as jnp
import numpy as np
from jax import lax
from jax.experimental import pallas as pl
from jax.experimental.pallas import tpu as pltpu

D_MODEL = 1024
XA_HEADS = 4
XA_HEAD_DIM = D_MODEL // XA_HEADS
CONV_WIDTH = 31
CONV_HALF = CONV_WIDTH // 2
D_CONV = D_MODEL // 2
D_FOURIER = D_MODEL - D_CONV
FOURIER_GROUPS = 4
FOURIER_GROUP_DIM = D_FOURIER // FOURIER_GROUPS
POOL_WINDOWS = (2, 4, 8, 16)
POOL_GROUP_DIM = D_MODEL // len(POOL_WINDOWS)
RMS_EPS = 1e-6
LN_EPS = 1e-5

SUBLANES = 8
LANES = 128
BF16_ROWS = 16
CONV_HALO = 16
POOL_HALO = 8
CONV_ROW_CHUNK = 64
VMEM_LIMIT_BYTES = 56 * 1024 * 1024
DFT_RADIX = 4
DFT_ROW_CHUNK = 32
DFT_ROW_TILE = 512

BF16 = jnp.bfloat16
F32 = jnp.float32


def _seq_tile(seq_len, want):
    tile = min(want, seq_len)
    assert seq_len % tile == 0 and tile % CONV_ROW_CHUNK == 0
    return tile


def _params(n_axes):
    return pltpu.CompilerParams(dimension_semantics=("parallel",) * n_axes,
                                vmem_limit_bytes=VMEM_LIMIT_BYTES)


def _resident(shape):
    zeros = (0,) * len(shape)
    return pl.BlockSpec(shape, lambda *_: zeros, pipeline_mode=pl.Buffered(1))


def _rms(xf, g):
    return xf * lax.rsqrt(jnp.mean(xf * xf, axis=-1, keepdims=True) + RMS_EPS) * g


def _dot(a, b):
    return jnp.dot(a, b, preferred_element_type=F32)


def _even_in_kernel(x_ref, g_ref, w_in_ref, cdft_ref, hglu_ref, fa_ref, fb_ref):
    h = _rms(x_ref[0], g_ref[...]).astype(BF16)
    u = _dot(h, w_in_ref[...])
    val, gate = u[:, :D_CONV], u[:, D_CONV:2 * D_CONV]
    hglu_ref[0] = (val * jax.nn.sigmoid(gate)).astype(BF16)
    uf = u[:, 2 * D_CONV:].astype(BF16)
    fa, fb = [], []
    for grp in range(FOURIER_GROUPS):
        c0 = grp * FOURIER_GROUP_DIM
        ab = _dot(uf[:, c0:c0 + FOURIER_GROUP_DIM], cdft_ref[...])
        fa.append(ab[:, :FOURIER_GROUP_DIM])
        fb.append(ab[:, FOURIER_GROUP_DIM:])
    fa_ref[0] = jnp.concatenate(fa, axis=-1).astype(BF16)
    fb_ref[0] = jnp.concatenate(fb, axis=-1).astype(BF16)


def _even_in(x, g, w_in, cdft, tm):
    b, s, d = x.shape
    d_in = w_in.shape[1]
    tile = lambda width: pl.BlockSpec((1, tm, width), lambda i, j: (i, j, 0))
    out = jax.ShapeDtypeStruct((b, s, D_CONV), BF16)
    return pl.pallas_call(
        _even_in_kernel,
        grid=(b, s // tm),
        in_specs=[tile(d), _resident((1, d)), _resident((d, d_in)),
                  _resident((FOURIER_GROUP_DIM, 2 * FOURIER_GROUP_DIM))],
        out_specs=[tile(D_CONV), tile(D_FOURIER), tile(D_FOURIER)],
        out_shape=[out, out, out],
        compiler_params=_params(2),
        name="even_in",
    )(x, g, w_in, cdft)


def _seq_dft_kernel(a_ref, b_ref, tw_ref, m_ref, f_ref, z_ref, nat_ref):
    s, c = a_ref.shape[1], a_ref.shape[2]
    q = s // DFT_RADIX
    tk = f_ref.shape[1] // DFT_RADIX
    rb = DFT_ROW_CHUNK

    @pl.when(pl.program_id(1) == 0)
    def _():
        def chunk(i, carry):
            r0 = pl.multiple_of(i * rb, rb)
            for l0 in range(0, c, LANES):
                lanes = slice(l0, l0 + LANES)
                a = [a_ref[0, pl.ds(n1 * q + r0, rb), lanes].astype(F32) for n1 in range(DFT_RADIX)]
                b = [b_ref[0, pl.ds(n1 * q + r0, rb), lanes].astype(F32) for n1 in range(DFT_RADIX)]
                sa02, da02, sa13, da13 = a[0] + a[2], a[0] - a[2], a[1] + a[3], a[1] - a[3]
                sb02, db02, sb13, db13 = b[0] + b[2], b[0] - b[2], b[1] + b[3], b[1] - b[3]
                re = [sa02 + sa13, da02 - db13, sa02 - sa13, da02 + db13]
                neg = [sb02 + sb13, db02 + da13, sb02 - sb13, db02 - da13]
                for k1 in range(DFT_RADIX):
                    zr, zn = re[k1], neg[k1]
                    if k1:
                        cs = tw_ref[k1 - 1, pl.ds(r0, rb), :]
                        sn = tw_ref[DFT_RADIX - 1 + k1 - 1, pl.ds(r0, rb), :]
                        zr, zn = cs * re[k1] - sn * neg[k1], cs * neg[k1] + sn * re[k1]
                    z_ref[k1, pl.ds(r0, rb), lanes] = zr.astype(BF16)
                    z_ref[k1, pl.ds(q + r0, rb), lanes] = zn.astype(BF16)
            return carry

        lax.fori_loop(0, q // rb, chunk, 0)

    rows = m_ref[pl.ds(pl.multiple_of(pl.program_id(1) * tk, tk), tk), :]
    for k1 in range(DFT_RADIX):
        res = _dot(rows, z_ref[k1])
        for blk in range(c // LANES):
            nat_ref[blk, pl.ds(k1, tk, stride=DFT_RADIX), :] = res[:, blk * LANES:(blk + 1) * LANES]
    for blk in range(c // LANES):
        f_ref[0, :, blk * LANES:(blk + 1) * LANES] = nat_ref[blk].astype(BF16)


def _seq_dft(tables, fa, fb):
    tw, m = tables
    b, s, c = fa.shape
    q = s // DFT_RADIX
    tk = min(q, DFT_ROW_TILE)
    full = pl.BlockSpec((1, s, c), lambda i, j: (i, 0, 0))
    return pl.pallas_call(
        _seq_dft_kernel,
        grid=(b, q // tk),
        in_specs=[full, full, _resident(tw.shape), _resident(m.shape)],
        out_specs=pl.BlockSpec((1, DFT_RADIX * tk, c), lambda i, j: (i, j, 0)),
        out_shape=jax.ShapeDtypeStruct((b, s, c), BF16),
        scratch_shapes=[pltpu.VMEM((DFT_RADIX, 2 * q, c), BF16),
                        pltpu.VMEM((c // LANES, DFT_RADIX * tk, LANES), F32)],
        compiler_params=pltpu.CompilerParams(dimension_semantics=("arbitrary", "arbitrary"),
                                             vmem_limit_bytes=VMEM_LIMIT_BYTES),
        name="seq_dft",
    )(fa, fb, tw, m)


def _even_out_kernel(x_ref, prev_ref, cur_ref, next_ref, f_ref, cw_ref, cb_ref, lg_ref, lb_ref,
                     w_out_ref, o_ref, win_ref, conv_ref):
    t, nt = pl.program_id(1), pl.num_programs(1)
    tm = cur_ref.shape[1]
    prev = jnp.where(t > 0, prev_ref[0].astype(F32), 0.0)
    nxt = jnp.where(t < nt - 1, next_ref[0].astype(F32), 0.0)
    for blk in range(D_CONV // LANES):
        lanes = slice(blk * LANES, (blk + 1) * LANES)
        win_ref[blk, 0:CONV_HALO, :] = prev[:, lanes]
        win_ref[blk, CONV_HALO:CONV_HALO + tm, :] = cur_ref[0, :, lanes].astype(F32)
        win_ref[blk, CONV_HALO + tm:, :] = nxt[:, lanes]

    rc = CONV_ROW_CHUNK
    first = CONV_HALO - CONV_HALF

    def chunk(c, carry):
        r0 = pl.multiple_of(c * rc, rc)
        for blk in range(D_CONV // LANES):
            lanes = slice(blk * LANES, (blk + 1) * LANES)
            acc = jnp.zeros((rc, LANES), F32)
            for j in range(CONV_WIDTH):
                acc = acc + cw_ref[j:j + 1, lanes] * win_ref[blk, pl.ds(r0 + first + j, rc), :]
            conv_ref[pl.ds(r0, rc), lanes] = acc
        return carry

    lax.fori_loop(0, tm // rc, chunk, 0)
    conv = conv_ref[...] + cb_ref[...]
    mu = jnp.mean(conv, axis=-1, keepdims=True)
    xc = conv - mu
    y = xc * lax.rsqrt(jnp.mean(xc * xc, axis=-1, keepdims=True) + LN_EPS) * lg_ref[...] + lb_ref[...]
    a = (y * jax.nn.sigmoid(y)).astype(BF16)
    cat = jnp.concatenate([a, f_ref[0]], axis=-1)
    o_ref[0] = x_ref[0] + _dot(cat, w_out_ref[...])


def _even_out(x, hglu, f, conv_w, conv_b, ln_g, ln_b, w_out, tm):
    b, s, d = x.shape
    per_halo = tm // CONV_HALO
    last_halo = s // CONV_HALO - 1
    tile = lambda width: pl.BlockSpec((1, tm, width), lambda i, j: (i, j, 0))
    prev = pl.BlockSpec((1, CONV_HALO, D_CONV), lambda i, j: (i, jnp.maximum(j * per_halo - 1, 0), 0))
    nxt = pl.BlockSpec((1, CONV_HALO, D_CONV),
                       lambda i, j: (i, jnp.minimum((j + 1) * per_halo, last_halo), 0))
    return pl.pallas_call(
        _even_out_kernel,
        grid=(b, s // tm),
        in_specs=[tile(d), prev, tile(D_CONV), nxt, tile(D_FOURIER),
                  _resident(conv_w.shape), _resident((1, D_CONV)), _resident((1, D_CONV)),
                  _resident((1, D_CONV)), _resident((d, d))],
        out_specs=tile(d),
        out_shape=jax.ShapeDtypeStruct((b, s, d), F32),
        scratch_shapes=[pltpu.VMEM((D_CONV // LANES, tm + 2 * CONV_HALO, LANES), F32),
                        pltpu.VMEM((tm, D_CONV), F32)],
        compiler_params=_params(2),
        name="even_out",
    )(x, hglu, hglu, hglu, f, conv_w, conv_b, ln_g, ln_b, w_out)


def _pool_mix(prev, x, nxt, g, w_ref, scale, t, seq_len):
    tm = x.shape[0]
    n = tm + 2 * POOL_HALO
    hwin = _rms(jnp.concatenate([prev, x, nxt], axis=0), g)
    pos = t * tm + lax.broadcasted_iota(jnp.int32, (tm, 1), 0)

    def ahead(v, k):
        return pltpu.roll(v, n - k, axis=0)

    def rows_from(v, k):
        return v[k:k + tm] if k % SUBLANES == 0 else ahead(v, k)[:tm]

    ys = []
    for grp, w in enumerate(POOL_WINDOWS):
        assert w <= 2 * POOL_HALO
        c0 = grp * POOL_GROUP_DIM
        hg = hwin[:, c0:c0 + POOL_GROUP_DIM]
        acc, width = hg, 1
        while 2 * width <= min(w, POOL_HALO):
            acc = acc + ahead(acc, width)
            width *= 2
        tot = rows_from(acc, POOL_HALO - w // 2)
        for k in range(1, w // width):
            tot = tot + rows_from(acc, POOL_HALO - w // 2 + k * width)
        cnt = jnp.minimum(pos + w // 2, seq_len) - jnp.maximum(pos - w // 2, 0)
        p = tot / cnt.astype(F32) - hg[POOL_HALO:POOL_HALO + tm]
        ys.append(_dot(p.astype(BF16), w_ref[grp]))
    return x + jnp.concatenate(ys, axis=-1) * scale


SCORE_SCALE_LOG2 = float(XA_HEAD_DIM ** -0.5 * np.log2(np.e))


def _kv_kernel(mem_ref, g_ref, wkv_ref, k_ref, v_ref):
    m = _rms(mem_ref[...], g_ref[...]).astype(BF16)
    kv = _dot(m, wkv_ref[...])
    k_ref[...] = (kv[:, :D_MODEL] * SCORE_SCALE_LOG2).astype(BF16)
    v_ref[...] = kv[:, D_MODEL:].astype(BF16)


def _kv(mem, g, wkv, tm):
    b, m, d = mem.shape
    tm = _seq_tile(b * m, tm)
    blk = pl.BlockSpec((tm, d), lambda i: (i, 0))
    out = jax.ShapeDtypeStruct((b * m, d), BF16)
    k, v = pl.pallas_call(
        _kv_kernel,
        grid=(b * m // tm,),
        in_specs=[blk, _resident((1, d)), _resident((d, 2 * d))],
        out_specs=[blk, blk],
        out_shape=[out, out],
        compiler_params=_params(1),
        name="mem_kv",
    )(mem.reshape(b * m, d), g, wkv)
    return k.reshape(b, m, d), v.reshape(b, m, d)


def _attend(x, g, wq_ref, k_ref, v_ref, wo_ref):
    h = _rms(x, g).astype(BF16)
    q = _dot(h, wq_ref[...]).astype(BF16)
    heads = []
    for hd in range(XA_HEADS):
        c0 = hd * XA_HEAD_DIM
        z = lax.dot_general(q[:, c0:c0 + XA_HEAD_DIM], k_ref[0, :, c0:c0 + XA_HEAD_DIM],
                            (((1,), (1,)), ((), ())), preferred_element_type=F32)
        e = jnp.exp2(z - jnp.max(z, axis=-1, keepdims=True))
        o = _dot(e.astype(BF16), v_ref[0, :, c0:c0 + XA_HEAD_DIM])
        heads.append((o / jnp.sum(e, axis=-1, keepdims=True)).astype(BF16))
    return x + _dot(jnp.concatenate(heads, axis=-1), wo_ref[...])


def _attn_kernel(x_ref, g_ref, wq_ref, k_ref, v_ref, wo_ref, o_ref):
    o_ref[0] = _attend(x_ref[0], g_ref[...], wq_ref, k_ref, v_ref, wo_ref)


def _pool_attn_kernel(prev_ref, x_ref, next_ref, gp_ref, wp_ref, scale_ref, g_ref, wq_ref, k_ref, v_ref,
                      wo_ref, o_ref, *, seq_len):
    t, nt = pl.program_id(1), pl.num_programs(1)
    prev = jnp.where(t > 0, prev_ref[0], 0.0)
    nxt = jnp.where(t < nt - 1, next_ref[0], 0.0)
    x = _pool_mix(prev, x_ref[0], nxt, gp_ref[...], wp_ref, scale_ref[...], t, seq_len)
    o_ref[0] = _attend(x, g_ref[...], wq_ref, k_ref, v_ref, wo_ref)


def _attn(x, g, wq, k, v, wo, tm, pool=None):
    b, s, d = x.shape
    m = k.shape[1]
    tile = pl.BlockSpec((1, tm, d), lambda i, j: (i, j, 0))
    memblk = pl.BlockSpec((1, m, d), lambda i, j: (i, 0, 0))
    attn_specs = [_resident((1, d)), _resident((d, d)), memblk, memblk, _resident((d, d))]
    attn_args = (g, wq, k, v, wo)
    if pool is None:
        body, specs, args = _attn_kernel, [tile], (x,)
    else:
        per_halo = tm // POOL_HALO
        last_halo = s // POOL_HALO - 1
        prev = pl.BlockSpec((1, POOL_HALO, d), lambda i, j: (i, jnp.maximum(j * per_halo - 1, 0), 0))
        nxt = pl.BlockSpec((1, POOL_HALO, d),
                           lambda i, j: (i, jnp.minimum((j + 1) * per_halo, last_halo), 0))
        gp, wp, scale = pool
        body = functools.partial(_pool_attn_kernel, seq_len=s)
        specs = [prev, tile, nxt, _resident((1, d)), _resident(wp.shape), _resident((1, d))]
        args = (x, x, x, gp, wp, scale)
    return pl.pallas_call(
        body,
        grid=(b, s // tm),
        in_specs=specs + attn_specs,
        out_specs=tile,
        out_shape=jax.ShapeDtypeStruct((b, s, d), F32),
        compiler_params=_params(2),
        name="cross_attn" if pool is None else "pool_cross_attn",
    )(*args, *attn_args)


def _ffn_chunks(d_ff):
    mxu = 256
    assert d_ff % mxu == 0
    n = d_ff // mxu
    sizes = [n // 2, n - n // 2] if n > 4 else [n]
    out, c0 = [], 0
    for sz in sizes:
        out.append((c0, sz * mxu))
        c0 += sz * mxu
    return out


def _ffn_kernel(x_ref, g_ref, wgu_ref, wd_ref, gf_ref, o_ref, *, final_norm):
    d_ff = wd_ref.shape[0]
    x = x_ref[...]
    h = _rms(x, g_ref[...]).astype(BF16)
    acc = x
    for c0, width in _ffn_chunks(d_ff):
        gate = _dot(h, wgu_ref[:, c0:c0 + width])
        up = _dot(h, wgu_ref[:, d_ff + c0:d_ff + c0 + width])
        act = (gate * jax.nn.sigmoid(gate) * up).astype(BF16)
        acc = acc + _dot(act, wd_ref[c0:c0 + width, :])
    o_ref[...] = _rms(acc, gf_ref[...]) if final_norm else acc


def _ffn(x, g, wgu, wd, g_final, tm, final_norm):
    n, d = x.shape
    d_ff = wd.shape[0]
    tile = pl.BlockSpec((tm, d), lambda i: (i, 0))
    return pl.pallas_call(
        functools.partial(_ffn_kernel, final_norm=final_norm),
        grid=(n // tm,),
        in_specs=[tile, _resident((1, d)), _resident((d, 2 * d_ff)), _resident((d_ff, d)),
                  _resident((1, d))],
        out_specs=tile,
        out_shape=jax.ShapeDtypeStruct((n, d), F32),
        compiler_params=_params(1),
        name="ffn",
    )(x, g, wgu, wd, g_final)


def _channel_dft():
    n = FOURIER_GROUP_DIM
    idx = (np.arange(n)[:, None] * np.arange(n)[None, :]) % n
    ang = 2.0 * np.pi * idx / n
    m = np.concatenate([np.cos(ang), np.sin(ang)], axis=1) / np.sqrt(n)
    return jnp.asarray(m, dtype=F32).astype(BF16)


def _sequence_dft_tables(s):
    q = s // DFT_RADIX
    n2 = np.arange(q)
    k1 = np.arange(1, DFT_RADIX)
    ang = ((k1[:, None] * n2[None, :]) % s) * (2.0 * np.pi / s)
    tw = jnp.asarray(np.concatenate([np.cos(ang), np.sin(ang)], axis=0), dtype=F32)
    tw = jnp.broadcast_to(tw[:, :, None], (2 * (DFT_RADIX - 1), q, LANES))
    ang = ((n2[:, None] * n2[None, :]) % q) * (2.0 * np.pi / q)
    m = np.concatenate([np.cos(ang), -np.sin(ang)], axis=1) * (s ** -0.5)
    return tw, jnp.asarray(m, dtype=F32).astype(BF16)


def _trunk(x, mem, p, tm=1024, tm_attn=1024, tm_ffn=1024):
    b, s, d = x.shape
    depth = p["norm_mix"].shape[0]
    tm = _seq_tile(s, tm)
    tm_attn = _seq_tile(s, tm_attn)
    row = lambda v: v.reshape(1, -1)
    for l in range(depth):
        if l % 2 == 0:
            e = l // 2
            hglu, fa, fb = _even_in(x, row(p["norm_mix"][l]), p["w_in_even"][e], p["cdft"], tm)
            f = _seq_dft(_sequence_dft_tables(s), fa, fb)
            x = _even_out(x, hglu, f, p["conv_w"][e], row(p["conv_b"][e]), row(p["conv_ln_g"][e]),
                          row(p["conv_ln_b"][e]), p["w_out_even"][e], tm)
            pool = None
        else:
            o = l // 2
            pool = (row(p["norm_mix"][l]), p["w_pool"][o], row(p["pool_scale"][o]))
        k, v = _kv(mem, row(p["norm_mem"][l]), p["xa_wkv"][l], tm_attn)
        x = _attn(x, row(p["norm_xa"][l]), p["xa_wq"][l], k, v, p["xa_wo"][l], tm_attn, pool)
        x = _ffn(x.reshape(b * s, d), row(p["norm_ffn"][l]), p["ffn_w_gate_up"][l],
                 p["ffn_w_down"][l], row(p["norm_final"]), _seq_tile(b * s, tm_ffn),
                 final_norm=(l == depth - 1)).reshape(b, s, d)
    return x


def _prepare(norm_mix, w_in_even, conv_w, conv_b, conv_ln_g, conv_ln_b, w_out_even, w_pool, pool_scale,
             norm_xa, norm_mem, xa_wq, xa_wkv, xa_wo, norm_ffn, ffn_w_gate_up, ffn_w_down, norm_final):
    conv_w = jnp.pad(conv_w, ((0, 0), (0, -conv_w.shape[1] % SUBLANES), (0, 0)))
    per_layer = lambda w: [w[l].astype(BF16) for l in range(w.shape[0])]
    return dict(
        norm_mix=norm_mix, w_in_even=per_layer(w_in_even), conv_w=conv_w, conv_b=conv_b,
        conv_ln_g=conv_ln_g, conv_ln_b=conv_ln_b, w_out_even=per_layer(w_out_even),
        w_pool=per_layer(w_pool), pool_scale=pool_scale, norm_xa=norm_xa, norm_mem=norm_mem,
        xa_wq=per_layer(xa_wq), xa_wkv=per_layer(xa_wkv), xa_wo=per_layer(xa_wo),
        norm_ffn=norm_ffn, ffn_w_gate_up=per_layer(ffn_w_gate_up),
        ffn_w_down=per_layer(ffn_w_down), norm_final=norm_final, cdft=_channel_dft())


def kernel(x_prompt, x_sample, mem_prompt, mem_sample, norm_mix, w_in_even, conv_w, conv_b, conv_ln_g,
           conv_ln_b, w_out_even, w_pool, pool_scale, norm_xa, norm_mem, xa_wq, xa_wkv, xa_wo, norm_ffn,
           ffn_w_gate_up, ffn_w_down, norm_final):
    p = _prepare(norm_mix, w_in_even, conv_w, conv_b, conv_ln_g, conv_ln_b, w_out_even, w_pool,
                 pool_scale, norm_xa, norm_mem, xa_wq, xa_wkv, xa_wo, norm_ffn, ffn_w_gate_up,
                 ffn_w_down, norm_final)
    return _trunk(x_prompt, mem_prompt, p), _trunk(x_sample, mem_sample, p)
```

```python
import functools

import jax
import jax.numpy as jnp
import numpy as np
from jax import lax
from jax.experimental import pallas as pl
from jax.experimental.pallas import tpu as pltpu

D_MODEL = 1024
XA_HEADS = 4
XA_HEAD_DIM = D_MODEL // XA_HEADS
CONV_WIDTH = 31
CONV_HALF = CONV_WIDTH // 2
D_CONV = D_MODEL // 2
D_FOURIER = D_MODEL - D_CONV
FOURIER_GROUPS = 4
FOURIER_GROUP_DIM = D_FOURIER // FOURIER_GROUPS
POOL_WINDOWS = (2, 4, 8, 16)
POOL_GROUP_DIM = D_MODEL // len(POOL_WINDOWS)
RMS_EPS = 1e-6
LN_EPS = 1e-5

SUBLANES = 8
LANES = 128
BF16_ROWS = 16
CONV_HALO = 16
POOL_HALO = 8
CONV_ROW_CHUNK = 64
VMEM_LIMIT_BYTES = 56 * 1024 * 1024
CAST_ROW_TILE = 256
DFT_RADIX = 4
DFT_ROW_CHUNK = 32
DFT_ROW_TILE = 512

BF16 = jnp.bfloat16
F32 = jnp.float32


def _seq_tile(seq_len, want):
    tile = min(want, seq_len)
    assert seq_len % tile == 0 and tile % CONV_ROW_CHUNK == 0
    return tile


def _params(n_axes):
    return pltpu.CompilerParams(dimension_semantics=("parallel",) * n_axes,
                                vmem_limit_bytes=VMEM_LIMIT_BYTES)


def _resident(shape):
    zeros = (0,) * len(shape)
    return pl.BlockSpec(shape, lambda *_: zeros, pipeline_mode=pl.Buffered(1))


def _layer_weight(stack, layer):
    zeros = (0,) * (stack.ndim - 1)
    spec = pl.BlockSpec((None,) + stack.shape[1:], lambda *_: (layer,) + zeros, pipeline_mode=pl.Buffered(1))
    return spec, stack


def _cast_kernel(w_ref, o_ref):
    o_ref[...] = w_ref[...].astype(BF16)


def _to_bf16(w):
    layers, rows, cols = w.shape
    tr = CAST_ROW_TILE
    assert rows % tr == 0
    blk = pl.BlockSpec((1, tr, cols), lambda i, j: (i, j, 0))
    return pl.pallas_call(
        _cast_kernel,
        grid=(layers, rows // tr),
        in_specs=[blk],
        out_specs=blk,
        out_shape=jax.ShapeDtypeStruct(w.shape, BF16),
        compiler_params=_params(2),
        name="cast_bf16",
    )(w)


def _rms(xf, g):
    return xf * lax.rsqrt(jnp.mean(xf * xf, axis=-1, keepdims=True) + RMS_EPS) * g


def _dot(a, b):
    return jnp.dot(a, b, preferred_element_type=F32)


def _even_in_kernel(x_ref, g_ref, w_in_ref, cdft_ref, hglu_ref, fa_ref, fb_ref):
    h = _rms(x_ref[0], g_ref[...]).astype(BF16)
    u = _dot(h, w_in_ref[...])
    val, gate = u[:, :D_CONV], u[:, D_CONV:2 * D_CONV]
    hglu_ref[0] = (val * jax.nn.sigmoid(gate)).astype(BF16)
    uf = u[:, 2 * D_CONV:].astype(BF16)
    fa, fb = [], []
    for grp in range(FOURIER_GROUPS):
        c0 = grp * FOURIER_GROUP_DIM
        ab = _dot(uf[:, c0:c0 + FOURIER_GROUP_DIM], cdft_ref[...])
        fa.append(ab[:, :FOURIER_GROUP_DIM])
        fb.append(ab[:, FOURIER_GROUP_DIM:])
    fa_ref[0] = jnp.concatenate(fa, axis=-1).astype(BF16)
    fb_ref[0] = jnp.concatenate(fb, axis=-1).astype(BF16)


def _even_in(x, g, w_in, cdft, tm):
    b, s, d = x.shape
    w_spec, w_in = _layer_weight(*w_in)
    tile = lambda width: pl.BlockSpec((1, tm, width), lambda i, j: (i, j, 0))
    out = jax.ShapeDtypeStruct((b, s, D_CONV), BF16)
    return pl.pallas_call(
        _even_in_kernel,
        grid=(b, s // tm),
        in_specs=[tile(d), _resident((1, d)), w_spec,
                  _resident((FOURIER_GROUP_DIM, 2 * FOURIER_GROUP_DIM))],
        out_specs=[tile(D_CONV), tile(D_FOURIER), tile(D_FOURIER)],
        out_shape=[out, out, out],
        compiler_params=_params(2),
        name="even_in",
    )(x, g, w_in, cdft)


def _seq_dft_kernel(a_ref, b_ref, tw_ref, m_ref, f_ref, z_ref, nat_ref):
    s, c = a_ref.shape[1], a_ref.shape[2]
    q = s // DFT_RADIX
    tk = f_ref.shape[1] // DFT_RADIX
    rb = DFT_ROW_CHUNK

    @pl.when(pl.program_id(1) == 0)
    def _():
        def chunk(i, carry):
            r0 = pl.multiple_of(i * rb, rb)
            for l0 in range(0, c, LANES):
                lanes = slice(l0, l0 + LANES)
                a = [a_ref[0, pl.ds(n1 * q + r0, rb), lanes].astype(F32) for n1 in range(DFT_RADIX)]
                b = [b_ref[0, pl.ds(n1 * q + r0, rb), lanes].astype(F32) for n1 in range(DFT_RADIX)]
                sa02, da02, sa13, da13 = a[0] + a[2], a[0] - a[2], a[1] + a[3], a[1] - a[3]
                sb02, db02, sb13, db13 = b[0] + b[2], b[0] - b[2], b[1] + b[3], b[1] - b[3]
                re = [sa02 + sa13, da02 - db13, sa02 - sa13, da02 + db13]
                neg = [sb02 + sb13, db02 + da13, sb02 - sb13, db02 - da13]
                for k1 in range(DFT_RADIX):
                    zr, zn = re[k1], neg[k1]
                    if k1:
                        cs = tw_ref[k1 - 1, pl.ds(r0, rb), :]
                        sn = tw_ref[DFT_RADIX - 1 + k1 - 1, pl.ds(r0, rb), :]
                        zr, zn = cs * re[k1] - sn * neg[k1], cs * neg[k1] + sn * re[k1]
                    z_ref[k1, pl.ds(r0, rb), lanes] = zr.astype(BF16)
                    z_ref[k1, pl.ds(q + r0, rb), lanes] = zn.astype(BF16)
            return carry

        lax.fori_loop(0, q // rb, chunk, 0)

    rows = m_ref[pl.ds(pl.multiple_of(pl.program_id(1) * tk, tk), tk), :]
    for k1 in range(DFT_RADIX):
        res = _dot(rows, z_ref[k1])
        for blk in range(c // LANES):
            nat_ref[blk, pl.ds(k1, tk, stride=DFT_RADIX), :] = res[:, blk * LANES:(blk + 1) * LANES]
    for blk in range(c // LANES):
        f_ref[0, :, blk * LANES:(blk + 1) * LANES] = nat_ref[blk].astype(BF16)


def _seq_dft(tables, fa, fb):
    tw, m = tables
    b, s, c = fa.shape
    q = s // DFT_RADIX
    tk = min(q, DFT_ROW_TILE)
    full = pl.BlockSpec((1, s, c), lambda i, j: (i, 0, 0))
    return pl.pallas_call(
        _seq_dft_kernel,
        grid=(b, q // tk),
        in_specs=[full, full, _resident(tw.shape), _resident(m.shape)],
        out_specs=pl.BlockSpec((1, DFT_RADIX * tk, c), lambda i, j: (i, j, 0)),
        out_shape=jax.ShapeDtypeStruct((b, s, c), BF16),
        scratch_shapes=[pltpu.VMEM((DFT_RADIX, 2 * q, c), BF16),
                        pltpu.VMEM((c // LANES, DFT_RADIX * tk, LANES), F32)],
        compiler_params=pltpu.CompilerParams(dimension_semantics=("arbitrary", "arbitrary"),
                                             vmem_limit_bytes=VMEM_LIMIT_BYTES),
        name="seq_dft",
    )(fa, fb, tw, m)


def _even_out_kernel(x_ref, prev_ref, cur_ref, next_ref, f_ref, cw_ref, cb_ref, lg_ref, lb_ref,
                     w_out_ref, o_ref, win_ref, conv_ref):
    t, nt = pl.program_id(1), pl.num_programs(1)
    tm = cur_ref.shape[1]
    prev = jnp.where(t > 0, prev_ref[0].astype(F32), 0.0)
    nxt = jnp.where(t < nt - 1, next_ref[0].astype(F32), 0.0)
    for blk in range(D_CONV // LANES):
        lanes = slice(blk * LANES, (blk + 1) * LANES)
        win_ref[blk, 0:CONV_HALO, :] = prev[:, lanes]
        win_ref[blk, CONV_HALO:CONV_HALO + tm, :] = cur_ref[0, :, lanes].astype(F32)
        win_ref[blk, CONV_HALO + tm:, :] = nxt[:, lanes]

    rc = CONV_ROW_CHUNK
    first = CONV_HALO - CONV_HALF

    def chunk(c, carry):
        r0 = pl.multiple_of(c * rc, rc)
        for blk in range(D_CONV // LANES):
            lanes = slice(blk * LANES, (blk + 1) * LANES)
            acc = jnp.zeros((rc, LANES), F32)
            for j in range(CONV_WIDTH):
                acc = acc + cw_ref[j:j + 1, lanes] * win_ref[blk, pl.ds(r0 + first + j, rc), :]
            conv_ref[pl.ds(r0, rc), lanes] = acc
        return carry

    lax.fori_loop(0, tm // rc, chunk, 0)
    conv = conv_ref[...] + cb_ref[...]
    mu = jnp.mean(conv, axis=-1, keepdims=True)
    xc = conv - mu
    y = xc * lax.rsqrt(jnp.mean(xc * xc, axis=-1, keepdims=True) + LN_EPS) * lg_ref[...] + lb_ref[...]
    a = (y * jax.nn.sigmoid(y)).astype(BF16)
    cat = jnp.concatenate([a, f_ref[0]], axis=-1)
    o_ref[0] = x_ref[0] + _dot(cat, w_out_ref[...])


def _even_out(x, hglu, f, conv_w, conv_b, ln_g, ln_b, w_out, tm):
    b, s, d = x.shape
    w_spec, w_out = _layer_weight(*w_out)
    per_halo = tm // CONV_HALO
    last_halo = s // CONV_HALO - 1
    tile = lambda width: pl.BlockSpec((1, tm, width), lambda i, j: (i, j, 0))
    prev = pl.BlockSpec((1, CONV_HALO, D_CONV), lambda i, j: (i, jnp.maximum(j * per_halo - 1, 0), 0))
    nxt = pl.BlockSpec((1, CONV_HALO, D_CONV),
                       lambda i, j: (i, jnp.minimum((j + 1) * per_halo, last_halo), 0))
    return pl.pallas_call(
        _even_out_kernel,
        grid=(b, s // tm),
        in_specs=[tile(d), prev, tile(D_CONV), nxt, tile(D_FOURIER),
                  _resident(conv_w.shape), _resident((1, D_CONV)), _resident((1, D_CONV)),
                  _resident((1, D_CONV)), w_spec],
        out_specs=tile(d),
        out_shape=jax.ShapeDtypeStruct((b, s, d), F32),
        scratch_shapes=[pltpu.VMEM((D_CONV // LANES, tm + 2 * CONV_HALO, LANES), F32),
                        pltpu.VMEM((tm, D_CONV), F32)],
        compiler_params=_params(2),
        name="even_out",
    )(x, hglu, hglu, hglu, f, conv_w, conv_b, ln_g, ln_b, w_out)


def _pool_mix(prev, x, nxt, g, w_ref, scale, t, seq_len):
    tm = x.shape[0]
    n = tm + 2 * POOL_HALO
    hwin = _rms(jnp.concatenate([prev, x, nxt], axis=0), g)

    def inv_count(first_row, w):
        pos = first_row + lax.broadcasted_iota(jnp.int32, (POOL_HALO, 1), 0)
        cnt = jnp.minimum(pos + w // 2, seq_len) - jnp.maximum(pos - w // 2, 0)
        return 1.0 / cnt.astype(F32)

    def ahead(v, k):
        return pltpu.roll(v, n - k, axis=0)

    def rows_from(v, k):
        return v[k:k + tm] if k % SUBLANES == 0 else ahead(v, k)[:tm]

    ys = []
    for grp, w in enumerate(POOL_WINDOWS):
        assert w <= 2 * POOL_HALO
        c0 = grp * POOL_GROUP_DIM
        hg = hwin[:, c0:c0 + POOL_GROUP_DIM]
        acc, width = hg, 1
        while 2 * width <= min(w, POOL_HALO):
            acc = acc + ahead(acc, width)
            width *= 2
        tot = rows_from(acc, POOL_HALO - w // 2)
        for k in range(1, w // width):
            tot = tot + rows_from(acc, POOL_HALO - w // 2 + k * width)
        head, tail = slice(0, POOL_HALO), slice(tm - POOL_HALO, tm)
        mean = jnp.concatenate([tot[head] * inv_count(t * tm, w), tot[POOL_HALO:tm - POOL_HALO] * (1.0 / w),
                                tot[tail] * inv_count(t * tm + tm - POOL_HALO, w)], axis=0)
        p = mean - hg[POOL_HALO:POOL_HALO + tm]
        ys.append(_dot(p.astype(BF16), w_ref[grp]))
    return x + jnp.concatenate(ys, axis=-1) * scale


SCORE_SCALE_LOG2 = float(XA_HEAD_DIM ** -0.5 * np.log2(np.e))


def _kv_kernel(mem_ref, g_ref, wkv_ref, k_ref, v_ref):
    m = _rms(mem_ref[...], g_ref[...]).astype(BF16)
    kv = _dot(m, wkv_ref[...])
    k_ref[...] = (kv[:, :D_MODEL] * SCORE_SCALE_LOG2).astype(BF16)
    v_ref[...] = kv[:, D_MODEL:].astype(BF16)


def _kv(mem, g, wkv, tm):
    b, m, d = mem.shape
    w_spec, wkv = _layer_weight(*wkv)
    tm = _seq_tile(b * m, tm)
    blk = pl.BlockSpec((tm, d), lambda i: (i, 0))
    out = jax.ShapeDtypeStruct((b * m, d), BF16)
    k, v = pl.pallas_call(
        _kv_kernel,
        grid=(b * m // tm,),
        in_specs=[blk, _resident((1, d)), w_spec],
        out_specs=[blk, blk],
        out_shape=[out, out],
        compiler_params=_params(1),
        name="mem_kv",
    )(mem.reshape(b * m, d), g, wkv)
    return k.reshape(b, m, d), v.reshape(b, m, d)


def _attend(x, g, wq_ref, k_ref, v_ref, wo_ref):
    h = _rms(x, g).astype(BF16)
    q = _dot(h, wq_ref[...]).astype(BF16)
    heads = []
    for hd in range(XA_HEADS):
        c0 = hd * XA_HEAD_DIM
        z = lax.dot_general(q[:, c0:c0 + XA_HEAD_DIM], k_ref[0, :, c0:c0 + XA_HEAD_DIM],
                            (((1,), (1,)), ((), ())), preferred_element_type=F32)
        e = jnp.exp2(z - jnp.max(z, axis=-1, keepdims=True))
        o = _dot(e.astype(BF16), v_ref[0, :, c0:c0 + XA_HEAD_DIM])
        heads.append((o / jnp.sum(e, axis=-1, keepdims=True)).astype(BF16))
    return x + _dot(jnp.concatenate(heads, axis=-1), wo_ref[...])


def _attn_kernel(x_ref, g_ref, wq_ref, k_ref, v_ref, wo_ref, o_ref):
    o_ref[0] = _attend(x_ref[0], g_ref[...], wq_ref, k_ref, v_ref, wo_ref)


def _pool_attn_kernel(prev_ref, x_ref, next_ref, gp_ref, wp_ref, scale_ref, g_ref, wq_ref, k_ref, v_ref,
                      wo_ref, o_ref, *, seq_len):
    t, nt = pl.program_id(1), pl.num_programs(1)
    prev = jnp.where(t > 0, prev_ref[0], 0.0)
    nxt = jnp.where(t < nt - 1, next_ref[0], 0.0)
    x = _pool_mix(prev, x_ref[0], nxt, gp_ref[...], wp_ref, scale_ref[...], t, seq_len)
    o_ref[0] = _attend(x, g_ref[...], wq_ref, k_ref, v_ref, wo_ref)


def _attn(x, g, wq, k, v, wo, tm, pool=None):
    b, s, d = x.shape
    m = k.shape[1]
    tile = pl.BlockSpec((1, tm, d), lambda i, j: (i, j, 0))
    memblk = pl.BlockSpec((1, m, d), lambda i, j: (i, 0, 0))
    (wq_spec, wq), (wo_spec, wo) = _layer_weight(*wq), _layer_weight(*wo)
    attn_specs = [_resident((1, d)), wq_spec, memblk, memblk, wo_spec]
    attn_args = (g, wq, k, v, wo)
    if pool is None:
        body, specs, args = _attn_kernel, [tile], (x,)
    else:
        per_halo = tm // POOL_HALO
        last_halo = s // POOL_HALO - 1
        prev = pl.BlockSpec((1, POOL_HALO, d), lambda i, j: (i, jnp.maximum(j * per_halo - 1, 0), 0))
        nxt = pl.BlockSpec((1, POOL_HALO, d),
                           lambda i, j: (i, jnp.minimum((j + 1) * per_halo, last_halo), 0))
        gp, wp, scale = pool
        body = functools.partial(_pool_attn_kernel, seq_len=s)
        specs = [prev, tile, nxt, _resident((1, d)), _resident(wp.shape), _resident((1, d))]
        args = (x, x, x, gp, wp, scale)
    return pl.pallas_call(
        body,
        grid=(b, s // tm),
        in_specs=specs + attn_specs,
        out_specs=tile,
        out_shape=jax.ShapeDtypeStruct((b, s, d), F32),
        compiler_params=_params(2),
        name="cross_attn" if pool is None else "pool_cross_attn",
    )(*args, *attn_args)


def _ffn_chunks(d_ff):
    mxu = 256
    assert d_ff % mxu == 0
    n = d_ff // mxu
    sizes = [n // 2, n - n // 2] if n > 4 else [n]
    out, c0 = [], 0
    for sz in sizes:
        out.append((c0, sz * mxu))
        c0 += sz * mxu
    return out


def _ffn_kernel(x_ref, g_ref, wgu_ref, wd_ref, gf_ref, o_ref, *, final_norm):
    d_ff = wd_ref.shape[0]
    x = x_ref[...]
    h = _rms(x, g_ref[...]).astype(BF16)
    acc = x
    for c0, width in _ffn_chunks(d_ff):
        gate = _dot(h, wgu_ref[:, c0:c0 + width])
        up = _dot(h, wgu_ref[:, d_ff + c0:d_ff + c0 + width])
        act = (gate * jax.nn.sigmoid(gate) * up).astype(BF16)
        acc = acc + _dot(act, wd_ref[c0:c0 + width, :])
    o_ref[...] = _rms(acc, gf_ref[...]) if final_norm else acc


def _ffn(x, g, wgu, wd, g_final, tm, final_norm):
    n, d = x.shape
    (wgu_spec, wgu), (wd_spec, wd) = _layer_weight(*wgu), _layer_weight(*wd)
    tile = pl.BlockSpec((tm, d), lambda i: (i, 0))
    return pl.pallas_call(
        functools.partial(_ffn_kernel, final_norm=final_norm),
        grid=(n // tm,),
        in_specs=[tile, _resident((1, d)), wgu_spec, wd_spec, _resident((1, d))],
        out_specs=tile,
        out_shape=jax.ShapeDtypeStruct((n, d), F32),
        compiler_params=_params(1),
        name="ffn",
    )(x, g, wgu, wd, g_final)


def _channel_dft():
    n = FOURIER_GROUP_DIM
    idx = (np.arange(n)[:, None] * np.arange(n)[None, :]) % n
    ang = 2.0 * np.pi * idx / n
    m = np.concatenate([np.cos(ang), np.sin(ang)], axis=1) / np.sqrt(n)
    return jnp.asarray(m, dtype=F32).astype(BF16)


def _sequence_dft_tables(s):
    q = s // DFT_RADIX
    n2 = np.arange(q)
    k1 = np.arange(1, DFT_RADIX)
    ang = ((k1[:, None] * n2[None, :]) % s) * (2.0 * np.pi / s)
    tw = jnp.asarray(np.concatenate([np.cos(ang), np.sin(ang)], axis=0), dtype=F32)
    tw = jnp.broadcast_to(tw[:, :, None], (2 * (DFT_RADIX - 1), q, LANES))
    ang = ((n2[:, None] * n2[None, :]) % q) * (2.0 * np.pi / q)
    m = np.concatenate([np.cos(ang), -np.sin(ang)], axis=1) * (s ** -0.5)
    return tw, jnp.asarray(m, dtype=F32).astype(BF16)


def _trunk(x, mem, p, tm=1024, tm_attn=1024, tm_ffn=1024):
    b, s, d = x.shape
    depth = p["norm_mix"].shape[0]
    tm = _seq_tile(s, tm)
    tm_attn = _seq_tile(s, tm_attn)
    row = lambda v: v.reshape(1, -1)
    for l in range(depth):
        if l % 2 == 0:
            e = l // 2
            hglu, fa, fb = _even_in(x, row(p["norm_mix"][l]), (p["w_in_even"], e), p["cdft"], tm)
            f = _seq_dft(_sequence_dft_tables(s), fa, fb)
            x = _even_out(x, hglu, f, p["conv_w"][e], row(p["conv_b"][e]), row(p["conv_ln_g"][e]),
                          row(p["conv_ln_b"][e]), (p["w_out_even"], e), tm)
            pool = None
        else:
            o = l // 2
            pool = (row(p["norm_mix"][l]), p["w_pool"][o], row(p["pool_scale"][o]))
        k, v = _kv(mem, row(p["norm_mem"][l]), (p["xa_wkv"], l), tm_attn)
        x = _attn(x, row(p["norm_xa"][l]), (p["xa_wq"], l), k, v, (p["xa_wo"], l), tm_attn, pool)
        x = _ffn(x.reshape(b * s, d), row(p["norm_ffn"][l]), (p["ffn_w_gate_up"], l),
                 (p["ffn_w_down"], l), row(p["norm_final"]), _seq_tile(b * s, tm_ffn),
                 final_norm=(l == depth - 1)).reshape(b, s, d)
    return x


def _prepare(norm_mix, w_in_even, conv_w, conv_b, conv_ln_g, conv_ln_b, w_out_even, w_pool, pool_scale,
             norm_xa, norm_mem, xa_wq, xa_wkv, xa_wo, norm_ffn, ffn_w_gate_up, ffn_w_down, norm_final):
    conv_w = jnp.pad(conv_w, ((0, 0), (0, -conv_w.shape[1] % SUBLANES), (0, 0)))
    return dict(
        norm_mix=norm_mix, w_in_even=_to_bf16(w_in_even), conv_w=conv_w, conv_b=conv_b,
        conv_ln_g=conv_ln_g, conv_ln_b=conv_ln_b, w_out_even=_to_bf16(w_out_even),
        w_pool=w_pool.astype(BF16), pool_scale=pool_scale, norm_xa=norm_xa, norm_mem=norm_mem,
        xa_wq=_to_bf16(xa_wq), xa_wkv=_to_bf16(xa_wkv), xa_wo=_to_bf16(xa_wo),
        norm_ffn=norm_ffn, ffn_w_gate_up=_to_bf16(ffn_w_gate_up),
        ffn_w_down=_to_bf16(ffn_w_down), norm_final=norm_final, cdft=_channel_dft())


def kernel(x_prompt, x_sample, mem_prompt, mem_sample, norm_mix, w_in_even, conv_w, conv_b, conv_ln_g,
           conv_ln_b, w_out_even, w_pool, pool_scale, norm_xa, norm_mem, xa_wq, xa_wkv, xa_wo, norm_ffn,
           ffn_w_gate_up, ffn_w_down, norm_final):
    p = _prepare(norm_mix, w_in_even, conv_w, conv_b, conv_ln_g, conv_ln_b, w_out_even, w_pool,
                 pool_scale, norm_xa, norm_mem, xa_wq, xa_wkv, xa_wo, norm_ffn, ffn_w_gate_up,
                 ffn_w_down, norm_final)
    return _trunk(x_prompt, mem_prompt, p), _trunk(x_sample, mem_sample, p)
```

```python
import functools

import jax
import jax.numpy as jnp
import numpy as np
from jax import lax
from jax.experimental import pallas as pl
from jax.experimental.pallas import tpu as pltpu

D_MODEL = 1024
XA_HEADS = 4
XA_HEAD_DIM = D_MODEL // XA_HEADS
CONV_WIDTH = 31
CONV_HALF = CONV_WIDTH // 2
D_CONV = D_MODEL // 2
D_FOURIER = D_MODEL - D_CONV
FOURIER_GROUPS = 4
FOURIER_GROUP_DIM = D_FOURIER // FOURIER_GROUPS
POOL_WINDOWS = (2, 4, 8, 16)
POOL_GROUP_DIM = D_MODEL // len(POOL_WINDOWS)
RMS_EPS = 1e-6
LN_EPS = 1e-5

SUBLANES = 8
LANES = 128
BF16_ROWS = 16
CONV_HALO = 16
POOL_HALO = 8
CONV_ROW_CHUNK = 128
VMEM_LIMIT_BYTES = 56 * 1024 * 1024
CAST_ROW_TILE = 256
DFT_RADIX = 4
DFT_ROW_CHUNK = 32
DFT_ROW_TILE = 512

BF16 = jnp.bfloat16
F32 = jnp.float32


def _seq_tile(seq_len, want):
    tile = min(want, seq_len)
    assert seq_len % tile == 0 and tile % CONV_ROW_CHUNK == 0
    return tile


def _params(n_axes):
    return pltpu.CompilerParams(dimension_semantics=("parallel",) * n_axes,
                                vmem_limit_bytes=VMEM_LIMIT_BYTES)


def _resident(shape):
    zeros = (0,) * len(shape)
    return pl.BlockSpec(shape, lambda *_: zeros, pipeline_mode=pl.Buffered(1))


def _layer_weight(stack, layer):
    zeros = (0,) * (stack.ndim - 1)
    spec = pl.BlockSpec((None,) + stack.shape[1:], lambda *_: (layer,) + zeros, pipeline_mode=pl.Buffered(1))
    return spec, stack


def _cast_kernel(w_ref, o_ref):
    o_ref[...] = w_ref[...].astype(BF16)


def _to_bf16(w):
    layers, rows, cols = w.shape
    tr = CAST_ROW_TILE
    assert rows % tr == 0
    blk = pl.BlockSpec((1, tr, cols), lambda i, j: (i, j, 0))
    return pl.pallas_call(
        _cast_kernel,
        grid=(layers, rows // tr),
        in_specs=[blk],
        out_specs=blk,
        out_shape=jax.ShapeDtypeStruct(w.shape, BF16),
        compiler_params=_params(2),
        name="cast_bf16",
    )(w)


def _rms(xf, g):
    return xf * lax.rsqrt(jnp.mean(xf * xf, axis=-1, keepdims=True) + RMS_EPS) * g


def _dot(a, b):
    return jnp.dot(a, b, preferred_element_type=F32)


def _even_in_kernel(x_ref, g_ref, w_in_ref, cdft_ref, hglu_ref, fa_ref, fb_ref):
    h = _rms(x_ref[0], g_ref[...]).astype(BF16)
    u = _dot(h, w_in_ref[...])
    val, gate = u[:, :D_CONV], u[:, D_CONV:2 * D_CONV]
    hglu_ref[0] = (val * jax.nn.sigmoid(gate)).astype(BF16)
    uf = u[:, 2 * D_CONV:].astype(BF16)
    fa, fb = [], []
    for grp in range(FOURIER_GROUPS):
        c0 = grp * FOURIER_GROUP_DIM
        ab = _dot(uf[:, c0:c0 + FOURIER_GROUP_DIM], cdft_ref[...])
        fa.append(ab[:, :FOURIER_GROUP_DIM])
        fb.append(ab[:, FOURIER_GROUP_DIM:])
    fa_ref[0] = jnp.concatenate(fa, axis=-1).astype(BF16)
    fb_ref[0] = jnp.concatenate(fb, axis=-1).astype(BF16)


def _even_in(x, g, w_in, cdft, tm):
    b, s, d = x.shape
    w_spec, w_in = _layer_weight(*w_in)
    tile = lambda width: pl.BlockSpec((1, tm, width), lambda i, j: (i, j, 0))
    out = jax.ShapeDtypeStruct((b, s, D_CONV), BF16)
    return pl.pallas_call(
        _even_in_kernel,
        grid=(b, s // tm),
        in_specs=[tile(d), _resident((1, d)), w_spec,
                  _resident((FOURIER_GROUP_DIM, 2 * FOURIER_GROUP_DIM))],
        out_specs=[tile(D_CONV), tile(D_FOURIER), tile(D_FOURIER)],
        out_shape=[out, out, out],
        compiler_params=_params(2),
        name="even_in",
    )(x, g, w_in, cdft)


def _seq_dft_kernel(a_ref, b_ref, tw_ref, m_ref, f_ref, z_ref, nat_ref):
    s, c = a_ref.shape[1], a_ref.shape[2]
    q = s // DFT_RADIX
    tk = f_ref.shape[1] // DFT_RADIX
    rb = DFT_ROW_CHUNK

    @pl.when(pl.program_id(1) == 0)
    def _():
        def chunk(i, carry):
            r0 = pl.multiple_of(i * rb, rb)
            for l0 in range(0, c, LANES):
                lanes = slice(l0, l0 + LANES)
                a = [a_ref[0, pl.ds(n1 * q + r0, rb), lanes].astype(F32) for n1 in range(DFT_RADIX)]
                b = [b_ref[0, pl.ds(n1 * q + r0, rb), lanes].astype(F32) for n1 in range(DFT_RADIX)]
                sa02, da02, sa13, da13 = a[0] + a[2], a[0] - a[2], a[1] + a[3], a[1] - a[3]
                sb02, db02, sb13, db13 = b[0] + b[2], b[0] - b[2], b[1] + b[3], b[1] - b[3]
                re = [sa02 + sa13, da02 - db13, sa02 - sa13, da02 + db13]
                neg = [sb02 + sb13, db02 + da13, sb02 - sb13, db02 - da13]
                for k1 in range(DFT_RADIX):
                    zr, zn = re[k1], neg[k1]
                    if k1:
                        cs = tw_ref[k1 - 1, pl.ds(r0, rb), :]
                        sn = tw_ref[DFT_RADIX - 1 + k1 - 1, pl.ds(r0, rb), :]
                        zr, zn = cs * re[k1] - sn * neg[k1], cs * neg[k1] + sn * re[k1]
                    z_ref[k1, pl.ds(r0, rb), lanes] = zr.astype(BF16)
                    z_ref[k1, pl.ds(q + r0, rb), lanes] = zn.astype(BF16)
            return carry

        lax.fori_loop(0, q // rb, chunk, 0)

    rows = m_ref[pl.ds(pl.multiple_of(pl.program_id(1) * tk, tk), tk), :]
    for k1 in range(DFT_RADIX):
        res = _dot(rows, z_ref[k1])
        for blk in range(c // LANES):
            nat_ref[blk, pl.ds(k1, tk, stride=DFT_RADIX), :] = res[:, blk * LANES:(blk + 1) * LANES]
    for blk in range(c // LANES):
        f_ref[0, :, blk * LANES:(blk + 1) * LANES] = nat_ref[blk].astype(BF16)


def _seq_dft(tables, fa, fb):
    tw, m = tables
    b, s, c = fa.shape
    q = s // DFT_RADIX
    tk = min(q, DFT_ROW_TILE)
    full = pl.BlockSpec((1, s, c), lambda i, j: (i, 0, 0))
    return pl.pallas_call(
        _seq_dft_kernel,
        grid=(b, q // tk),
        in_specs=[full, full, _resident(tw.shape), _resident(m.shape)],
        out_specs=pl.BlockSpec((1, DFT_RADIX * tk, c), lambda i, j: (i, j, 0)),
        out_shape=jax.ShapeDtypeStruct((b, s, c), BF16),
        scratch_shapes=[pltpu.VMEM((DFT_RADIX, 2 * q, c), BF16),
                        pltpu.VMEM((c // LANES, DFT_RADIX * tk, LANES), F32)],
        compiler_params=pltpu.CompilerParams(dimension_semantics=("arbitrary", "arbitrary"),
                                             vmem_limit_bytes=VMEM_LIMIT_BYTES),
        name="seq_dft",
    )(fa, fb, tw, m)


def _even_out_kernel(x_ref, prev_ref, cur_ref, next_ref, f_ref, cw_ref, cb_ref, lg_ref, lb_ref,
                     w_out_ref, o_ref, win_ref, conv_ref):
    t, nt = pl.program_id(1), pl.num_programs(1)
    tm = cur_ref.shape[1]
    prev = jnp.where(t > 0, prev_ref[0].astype(F32), 0.0)
    nxt = jnp.where(t < nt - 1, next_ref[0].astype(F32), 0.0)
    for blk in range(D_CONV // LANES):
        lanes = slice(blk * LANES, (blk + 1) * LANES)
        win_ref[blk, 0:CONV_HALO, :] = prev[:, lanes]
        win_ref[blk, CONV_HALO:CONV_HALO + tm, :] = cur_ref[0, :, lanes].astype(F32)
        win_ref[blk, CONV_HALO + tm:, :] = nxt[:, lanes]

    rc = CONV_ROW_CHUNK
    first = CONV_HALO - CONV_HALF

    def chunk(c, carry):
        r0 = pl.multiple_of(c * rc, rc)
        for blk in range(D_CONV // LANES):
            lanes = slice(blk * LANES, (blk + 1) * LANES)
            acc = jnp.zeros((rc, LANES), F32)
            for j in range(CONV_WIDTH):
                acc = acc + cw_ref[j:j + 1, lanes] * win_ref[blk, pl.ds(r0 + first + j, rc), :]
            conv_ref[pl.ds(r0, rc), lanes] = acc
        return carry

    lax.fori_loop(0, tm // rc, chunk, 0)
    conv = conv_ref[...] + cb_ref[...]
    mu = jnp.mean(conv, axis=-1, keepdims=True)
    xc = conv - mu
    y = xc * lax.rsqrt(jnp.mean(xc * xc, axis=-1, keepdims=True) + LN_EPS) * lg_ref[...] + lb_ref[...]
    a = (y * jax.nn.sigmoid(y)).astype(BF16)
    cat = jnp.concatenate([a, f_ref[0]], axis=-1)
    o_ref[0] = x_ref[0] + _dot(cat, w_out_ref[...])


def _even_out(x, hglu, f, conv_w, conv_b, ln_g, ln_b, w_out, tm):
    b, s, d = x.shape
    w_spec, w_out = _layer_weight(*w_out)
    per_halo = tm // CONV_HALO
    last_halo = s // CONV_HALO - 1
    tile = lambda width: pl.BlockSpec((1, tm, width), lambda i, j: (i, j, 0))
    prev = pl.BlockSpec((1, CONV_HALO, D_CONV), lambda i, j: (i, jnp.maximum(j * per_halo - 1, 0), 0))
    nxt = pl.BlockSpec((1, CONV_HALO, D_CONV),
                       lambda i, j: (i, jnp.minimum((j + 1) * per_halo, last_halo), 0))
    return pl.pallas_call(
        _even_out_kernel,
        grid=(b, s // tm),
        in_specs=[tile(d), prev, tile(D_CONV), nxt, tile(D_FOURIER),
                  _resident(conv_w.shape), _resident((1, D_CONV)), _resident((1, D_CONV)),
                  _resident((1, D_CONV)), w_spec],
        out_specs=tile(d),
        out_shape=jax.ShapeDtypeStruct((b, s, d), F32),
        scratch_shapes=[pltpu.VMEM((D_CONV // LANES, tm + 2 * CONV_HALO, LANES), F32),
                        pltpu.VMEM((tm, D_CONV), F32)],
        compiler_params=_params(2),
        name="even_out",
    )(x, hglu, hglu, hglu, f, conv_w, conv_b, ln_g, ln_b, w_out)


def _pool_mix(prev, x, nxt, g, w_ref, scale, t, seq_len):
    tm = x.shape[0]
    n = tm + 2 * POOL_HALO
    hwin = _rms(jnp.concatenate([prev, x, nxt], axis=0), g)

    def inv_count(first_row, w):
        pos = first_row + lax.broadcasted_iota(jnp.int32, (POOL_HALO, 1), 0)
        cnt = jnp.minimum(pos + w // 2, seq_len) - jnp.maximum(pos - w // 2, 0)
        return 1.0 / cnt.astype(F32)

    def ahead(v, k):
        return pltpu.roll(v, n - k, axis=0)

    def rows_from(v, k):
        return v[k:k + tm] if k % SUBLANES == 0 else ahead(v, k)[:tm]

    ys = []
    for grp, w in enumerate(POOL_WINDOWS):
        assert w <= 2 * POOL_HALO
        c0 = grp * POOL_GROUP_DIM
        hg = hwin[:, c0:c0 + POOL_GROUP_DIM]
        acc, width = hg, 1
        while 2 * width <= min(w, POOL_HALO):
            acc = acc + ahead(acc, width)
            width *= 2
        tot = rows_from(acc, POOL_HALO - w // 2)
        for k in range(1, w // width):
            tot = tot + rows_from(acc, POOL_HALO - w // 2 + k * width)
        head, tail = slice(0, POOL_HALO), slice(tm - POOL_HALO, tm)
        mean = jnp.concatenate([tot[head] * inv_count(t * tm, w), tot[POOL_HALO:tm - POOL_HALO] * (1.0 / w),
                                tot[tail] * inv_count(t * tm + tm - POOL_HALO, w)], axis=0)
        p = mean - hg[POOL_HALO:POOL_HALO + tm]
        ys.append(_dot(p.astype(BF16), w_ref[grp]))
    return x + jnp.concatenate(ys, axis=-1) * scale


SCORE_SCALE_LOG2 = float(XA_HEAD_DIM ** -0.5 * np.log2(np.e))


def _project_memory(mem_ref, gm_ref, wkv_ref, k_ref, v_ref):
    @pl.when(pl.program_id(1) == 0)
    def _():
        m = _rms(mem_ref[0], gm_ref[...]).astype(BF16)
        kv = _dot(m, wkv_ref[...])
        k_ref[...] = (kv[:, :D_MODEL] * SCORE_SCALE_LOG2).astype(BF16)
        v_ref[...] = kv[:, D_MODEL:].astype(BF16)


def _attend(x, g, wq_ref, k_ref, v_ref, wo_ref):
    h = _rms(x, g).astype(BF16)
    q = _dot(h, wq_ref[...]).astype(BF16)
    heads = []
    for hd in range(XA_HEADS):
        c0 = hd * XA_HEAD_DIM
        z = lax.dot_general(q[:, c0:c0 + XA_HEAD_DIM], k_ref[:, c0:c0 + XA_HEAD_DIM],
                            (((1,), (1,)), ((), ())), preferred_element_type=F32)
        e = jnp.exp2(z - jnp.max(z, axis=-1, keepdims=True))
        o = _dot(e.astype(BF16), v_ref[:, c0:c0 + XA_HEAD_DIM])
        heads.append((o / jnp.sum(e, axis=-1, keepdims=True)).astype(BF16))
    return x + _dot(jnp.concatenate(heads, axis=-1), wo_ref[...])


def _attn_kernel(x_ref, g_ref, wq_ref, mem_ref, gm_ref, wkv_ref, wo_ref, o_ref, k_ref, v_ref):
    _project_memory(mem_ref, gm_ref, wkv_ref, k_ref, v_ref)
    o_ref[0] = _attend(x_ref[0], g_ref[...], wq_ref, k_ref, v_ref, wo_ref)


def _pool_attn_kernel(prev_ref, x_ref, next_ref, gp_ref, wp_ref, scale_ref, g_ref, wq_ref, mem_ref, gm_ref,
                      wkv_ref, wo_ref, o_ref, k_ref, v_ref, *, seq_len):
    t, nt = pl.program_id(1), pl.num_programs(1)
    _project_memory(mem_ref, gm_ref, wkv_ref, k_ref, v_ref)
    prev = jnp.where(t > 0, prev_ref[0], 0.0)
    nxt = jnp.where(t < nt - 1, next_ref[0], 0.0)
    x = _pool_mix(prev, x_ref[0], nxt, gp_ref[...], wp_ref, scale_ref[...], t, seq_len)
    o_ref[0] = _attend(x, g_ref[...], wq_ref, k_ref, v_ref, wo_ref)


def _attn(x, g, wq, mem, g_mem, wkv, wo, tm, pool=None):
    b, s, d = x.shape
    m = mem.shape[1]
    tile = pl.BlockSpec((1, tm, d), lambda i, j: (i, j, 0))
    memblk = pl.BlockSpec((1, m, d), lambda i, j: (i, 0, 0))
    (wq_spec, wq), (wkv_spec, wkv), (wo_spec, wo) = _layer_weight(*wq), _layer_weight(*wkv), _layer_weight(*wo)
    attn_specs = [_resident((1, d)), wq_spec, memblk, _resident((1, d)), wkv_spec, wo_spec]
    attn_args = (g, wq, mem, g_mem, wkv, wo)
    if pool is None:
        body, specs, args = _attn_kernel, [tile], (x,)
    else:
        per_halo = tm // POOL_HALO
        last_halo = s // POOL_HALO - 1
        prev = pl.BlockSpec((1, POOL_HALO, d), lambda i, j: (i, jnp.maximum(j * per_halo - 1, 0), 0))
        nxt = pl.BlockSpec((1, POOL_HALO, d),
                           lambda i, j: (i, jnp.minimum((j + 1) * per_halo, last_halo), 0))
        gp, wp, scale = pool
        body = functools.partial(_pool_attn_kernel, seq_len=s)
        specs = [prev, tile, nxt, _resident((1, d)), _resident(wp.shape), _resident((1, d))]
        args = (x, x, x, gp, wp, scale)
    return pl.pallas_call(
        body,
        grid=(b, s // tm),
        in_specs=specs + attn_specs,
        out_specs=tile,
        out_shape=jax.ShapeDtypeStruct((b, s, d), F32),
        scratch_shapes=[pltpu.VMEM((m, d), BF16), pltpu.VMEM((m, d), BF16)],
        compiler_params=pltpu.CompilerParams(dimension_semantics=("arbitrary", "arbitrary"),
                                             vmem_limit_bytes=VMEM_LIMIT_BYTES),
        name="cross_attn" if pool is None else "pool_cross_attn",
    )(*args, *attn_args)


def _ffn_chunks(d_ff):
    mxu = 256
    assert d_ff % mxu == 0
    n = d_ff // mxu
    sizes = [n // 2, n - n // 2] if n > 4 else [n]
    out, c0 = [], 0
    for sz in sizes:
        out.append((c0, sz * mxu))
        c0 += sz * mxu
    return out


def _ffn_kernel(x_ref, g_ref, wgu_ref, wd_ref, gf_ref, o_ref, *, final_norm):
    d_ff = wd_ref.shape[0]
    x = x_ref[...]
    h = _rms(x, g_ref[...]).astype(BF16)
    acc = x
    for c0, width in _ffn_chunks(d_ff):
        gate = _dot(h, wgu_ref[:, c0:c0 + width])
        up = _dot(h, wgu_ref[:, d_ff + c0:d_ff + c0 + width])
        act = (gate * jax.nn.sigmoid(gate) * up).astype(BF16)
        acc = acc + _dot(act, wd_ref[c0:c0 + width, :])
    o_ref[...] = _rms(acc, gf_ref[...]) if final_norm else acc


def _ffn(x, g, wgu, wd, g_final, tm, final_norm):
    n, d = x.shape
    (wgu_spec, wgu), (wd_spec, wd) = _layer_weight(*wgu), _layer_weight(*wd)
    tile = pl.BlockSpec((tm, d), lambda i: (i, 0))
    return pl.pallas_call(
        functools.partial(_ffn_kernel, final_norm=final_norm),
        grid=(n // tm,),
        in_specs=[tile, _resident((1, d)), wgu_spec, wd_spec, _resident((1, d))],
        out_specs=tile,
        out_shape=jax.ShapeDtypeStruct((n, d), F32),
        compiler_params=_params(1),
        name="ffn",
    )(x, g, wgu, wd, g_final)


def _channel_dft():
    n = FOURIER_GROUP_DIM
    idx = (np.arange(n)[:, None] * np.arange(n)[None, :]) % n
    ang = 2.0 * np.pi * idx / n
    m = np.concatenate([np.cos(ang), np.sin(ang)], axis=1) / np.sqrt(n)
    return jnp.asarray(m, dtype=F32).astype(BF16)


def _sequence_dft_tables(s):
    q = s // DFT_RADIX
    n2 = np.arange(q)
    k1 = np.arange(1, DFT_RADIX)
    ang = ((k1[:, None] * n2[None, :]) % s) * (2.0 * np.pi / s)
    tw = jnp.asarray(np.concatenate([np.cos(ang), np.sin(ang)], axis=0), dtype=F32)
    tw = jnp.broadcast_to(tw[:, :, None], (2 * (DFT_RADIX - 1), q, LANES))
    ang = ((n2[:, None] * n2[None, :]) % q) * (2.0 * np.pi / q)
    m = np.concatenate([np.cos(ang), -np.sin(ang)], axis=1) * (s ** -0.5)
    return tw, jnp.asarray(m, dtype=F32).astype(BF16)


def _trunk(x, mem, p, tm=1024, tm_attn=1024, tm_ffn=1024):
    b, s, d = x.shape
    depth = p["norm_mix"].shape[0]
    tm = _seq_tile(s, tm)
    tm_attn = _seq_tile(s, tm_attn)
    row = lambda v: v.reshape(1, -1)
    for l in range(depth):
        if l % 2 == 0:
            e = l // 2
            hglu, fa, fb = _even_in(x, row(p["norm_mix"][l]), (p["w_in_even"], e), p["cdft"], tm)
            f = _seq_dft(_sequence_dft_tables(s), fa, fb)
            x = _even_out(x, hglu, f, p["conv_w"][e], row(p["conv_b"][e]), row(p["conv_ln_g"][e]),
                          row(p["conv_ln_b"][e]), (p["w_out_even"], e), tm)
            pool = None
        else:
            o = l // 2
            pool = (row(p["norm_mix"][l]), p["w_pool"][o], row(p["pool_scale"][o]))
        x = _attn(x, row(p["norm_xa"][l]), (p["xa_wq"], l), mem, row(p["norm_mem"][l]), (p["xa_wkv"], l),
                  (p["xa_wo"], l), tm_attn, pool)
        x = _ffn(x.reshape(b * s, d), row(p["norm_ffn"][l]), (p["ffn_w_gate_up"], l),
                 (p["ffn_w_down"], l), row(p["norm_final"]), _seq_tile(b * s, tm_ffn),
                 final_norm=(l == depth - 1)).reshape(b, s, d)
    return x


def _prepare(norm_mix, w_in_even, conv_w, conv_b, conv_ln_g, conv_ln_b, w_out_even, w_pool, pool_scale,
             norm_xa, norm_mem, xa_wq, xa_wkv, xa_wo, norm_ffn, ffn_w_gate_up, ffn_w_down, norm_final):
    conv_w = jnp.pad(conv_w, ((0, 0), (0, -conv_w.shape[1] % SUBLANES), (0, 0)))
    return dict(
        norm_mix=norm_mix, w_in_even=_to_bf16(w_in_even), conv_w=conv_w, conv_b=conv_b,
        conv_ln_g=conv_ln_g, conv_ln_b=conv_ln_b, w_out_even=_to_bf16(w_out_even),
        w_pool=w_pool.astype(BF16), pool_scale=pool_scale, norm_xa=norm_xa, norm_mem=norm_mem,
        xa_wq=_to_bf16(xa_wq), xa_wkv=_to_bf16(xa_wkv), xa_wo=_to_bf16(xa_wo),
        norm_ffn=norm_ffn, ffn_w_gate_up=_to_bf16(ffn_w_gate_up),
        ffn_w_down=_to_bf16(ffn_w_down), norm_final=norm_final, cdft=_channel_dft())


def kernel(x_prompt, x_sample, mem_prompt, mem_sample, norm_mix, w_in_even, conv_w, conv_b, conv_ln_g,
           conv_ln_b, w_out_even, w_pool, pool_scale, norm_xa, norm_mem, xa_wq, xa_wkv, xa_wo, norm_ffn,
           ffn_w_gate_up, ffn_w_down, norm_final):
    p = _prepare(norm_mix, w_in_even, conv_w, conv_b, conv_ln_g, conv_ln_b, w_out_even, w_pool,
                 pool_scale, norm_xa, norm_mem, xa_wq, xa_wkv, xa_wo, norm_ffn, ffn_w_gate_up,
                 ffn_w_down, norm_final)
    return _trunk(x_prompt, mem_prompt, p), _trunk(x_sample, mem_sample, p)
```

```python
import functools

import jax
import jax.numpy as jnp
import numpy as np
from jax import lax
from jax.experimental import pallas as pl
from jax.experimental.pallas import tpu as pltpu

D_MODEL = 1024
XA_HEADS = 4
XA_HEAD_DIM = D_MODEL // XA_HEADS
CONV_WIDTH = 31
CONV_HALF = CONV_WIDTH // 2
D_CONV = D_MODEL // 2
D_FOURIER = D_MODEL - D_CONV
FOURIER_GROUPS = 4
FOURIER_GROUP_DIM = D_FOURIER // FOURIER_GROUPS
POOL_WINDOWS = (2, 4, 8, 16)
POOL_GROUP_DIM = D_MODEL // len(POOL_WINDOWS)
RMS_EPS = 1e-6
LN_EPS = 1e-5

SUBLANES = 8
LANES = 128
BF16_ROWS = 16
CONV_HALO = 16
POOL_HALO = 8
CONV_ROW_CHUNK = 128
VMEM_LIMIT_BYTES = 56 * 1024 * 1024
CAST_ROW_TILE = 256
DFT_RADIX = 4
DFT_ROW_CHUNK = 32
DFT_ROW_TILE = 512

BF16 = jnp.bfloat16
F32 = jnp.float32


def _seq_tile(seq_len, want):
    tile = min(want, seq_len)
    assert seq_len % tile == 0 and tile % CONV_ROW_CHUNK == 0
    return tile


def _params(n_axes):
    return pltpu.CompilerParams(dimension_semantics=("parallel",) * n_axes,
                                vmem_limit_bytes=VMEM_LIMIT_BYTES)


def _resident(shape):
    zeros = (0,) * len(shape)
    return pl.BlockSpec(shape, lambda *_: zeros, pipeline_mode=pl.Buffered(1))


def _layer_weight(stack, layer):
    zeros = (0,) * (stack.ndim - 1)
    spec = pl.BlockSpec((None,) + stack.shape[1:], lambda *_: (layer,) + zeros, pipeline_mode=pl.Buffered(1))
    return spec, stack


def _cast_kernel(w_ref, o_ref):
    o_ref[...] = w_ref[...].astype(BF16)


def _to_bf16(w):
    layers, rows, cols = w.shape
    tr = CAST_ROW_TILE
    assert rows % tr == 0
    blk = pl.BlockSpec((1, tr, cols), lambda i, j: (i, j, 0))
    return pl.pallas_call(
        _cast_kernel,
        grid=(layers, rows // tr),
        in_specs=[blk],
        out_specs=blk,
        out_shape=jax.ShapeDtypeStruct(w.shape, BF16),
        compiler_params=_params(2),
        name="cast_bf16",
    )(w)


def _rms(xf, g):
    return xf * lax.rsqrt(jnp.mean(xf * xf, axis=-1, keepdims=True) + RMS_EPS) * g


def _dot(a, b):
    return jnp.dot(a, b, preferred_element_type=F32)


def _even_in_kernel(x_ref, g_ref, w_in_ref, cdft_ref, hglu_ref, fa_ref, fb_ref):
    h = _rms(x_ref[0], g_ref[...]).astype(BF16)
    u = _dot(h, w_in_ref[...])
    val, gate = u[:, :D_CONV], u[:, D_CONV:2 * D_CONV]
    hglu_ref[0] = (val * jax.nn.sigmoid(gate)).astype(BF16)
    uf = u[:, 2 * D_CONV:].astype(BF16)
    fa, fb = [], []
    for grp in range(FOURIER_GROUPS):
        c0 = grp * FOURIER_GROUP_DIM
        ab = _dot(uf[:, c0:c0 + FOURIER_GROUP_DIM], cdft_ref[...])
        fa.append(ab[:, :FOURIER_GROUP_DIM])
        fb.append(ab[:, FOURIER_GROUP_DIM:])
    fa_ref[0] = jnp.concatenate(fa, axis=-1).astype(BF16)
    fb_ref[0] = jnp.concatenate(fb, axis=-1).astype(BF16)


def _even_in(x, g, w_in, cdft, tm):
    b, s, d = x.shape
    w_spec, w_in = _layer_weight(*w_in)
    tile = lambda width: pl.BlockSpec((1, tm, width), lambda i, j: (i, j, 0))
    out = jax.ShapeDtypeStruct((b, s, D_CONV), BF16)
    return pl.pallas_call(
        _even_in_kernel,
        grid=(b, s // tm),
        in_specs=[tile(d), _resident((1, d)), w_spec,
                  _resident((FOURIER_GROUP_DIM, 2 * FOURIER_GROUP_DIM))],
        out_specs=[tile(D_CONV), tile(D_FOURIER), tile(D_FOURIER)],
        out_shape=[out, out, out],
        compiler_params=_params(2),
        name="even_in",
    )(x, g, w_in, cdft)


def _seq_dft_kernel(a_ref, b_ref, tw_ref, m_ref, f_ref, z_ref, nat_ref):
    s, c = a_ref.shape[1], a_ref.shape[2]
    q = s // DFT_RADIX
    tk = f_ref.shape[1] // DFT_RADIX
    rb = DFT_ROW_CHUNK

    @pl.when(pl.program_id(1) == 0)
    def _():
        def chunk(i, carry):
            r0 = pl.multiple_of(i * rb, rb)
            for l0 in range(0, c, LANES):
                lanes = slice(l0, l0 + LANES)
                a = [a_ref[0, pl.ds(n1 * q + r0, rb), lanes].astype(F32) for n1 in range(DFT_RADIX)]
                b = [b_ref[0, pl.ds(n1 * q + r0, rb), lanes].astype(F32) for n1 in range(DFT_RADIX)]
                sa02, da02, sa13, da13 = a[0] + a[2], a[0] - a[2], a[1] + a[3], a[1] - a[3]
                sb02, db02, sb13, db13 = b[0] + b[2], b[0] - b[2], b[1] + b[3], b[1] - b[3]
                re = [sa02 + sa13, da02 - db13, sa02 - sa13, da02 + db13]
                neg = [sb02 + sb13, db02 + da13, sb02 - sb13, db02 - da13]
                for k1 in range(DFT_RADIX):
                    zr, zn = re[k1], neg[k1]
                    if k1:
                        cs = tw_ref[k1 - 1, pl.ds(r0, rb), :]
                        sn = tw_ref[DFT_RADIX - 1 + k1 - 1, pl.ds(r0, rb), :]
                        zr, zn = cs * re[k1] - sn * neg[k1], cs * neg[k1] + sn * re[k1]
                    z_ref[k1, pl.ds(r0, rb), lanes] = zr.astype(BF16)
                    z_ref[k1, pl.ds(q + r0, rb), lanes] = zn.astype(BF16)
            return carry

        lax.fori_loop(0, q // rb, chunk, 0)

    rows = m_ref[pl.ds(pl.multiple_of(pl.program_id(1) * tk, tk), tk), :]
    for k1 in range(DFT_RADIX):
        res = _dot(rows, z_ref[k1])
        for blk in range(c // LANES):
            nat_ref[blk, pl.ds(k1, tk, stride=DFT_RADIX), :] = res[:, blk * LANES:(blk + 1) * LANES]
    for blk in range(c // LANES):
        f_ref[0, :, blk * LANES:(blk + 1) * LANES] = nat_ref[blk].astype(BF16)


def _seq_dft(tables, fa, fb):
    tw, m = tables
    b, s, c = fa.shape
    q = s // DFT_RADIX
    tk = min(q, DFT_ROW_TILE)
    full = pl.BlockSpec((1, s, c), lambda i, j: (i, 0, 0))
    return pl.pallas_call(
        _seq_dft_kernel,
        grid=(b, q // tk),
        in_specs=[full, full, _resident(tw.shape), _resident(m.shape)],
        out_specs=pl.BlockSpec((1, DFT_RADIX * tk, c), lambda i, j: (i, j, 0)),
        out_shape=jax.ShapeDtypeStruct((b, s, c), BF16),
        scratch_shapes=[pltpu.VMEM((DFT_RADIX, 2 * q, c), BF16),
                        pltpu.VMEM((c // LANES, DFT_RADIX * tk, LANES), F32)],
        compiler_params=pltpu.CompilerParams(dimension_semantics=("arbitrary", "arbitrary"),
                                             vmem_limit_bytes=VMEM_LIMIT_BYTES),
        name="seq_dft",
    )(fa, fb, tw, m)


def _even_out_kernel(x_ref, prev_ref, cur_ref, next_ref, f_ref, cw_ref, cb_ref, lg_ref, lb_ref,
                     w_out_ref, o_ref, win_ref, conv_ref):
    t, nt = pl.program_id(1), pl.num_programs(1)
    tm = cur_ref.shape[1]
    prev = jnp.where(t > 0, prev_ref[0].astype(F32), 0.0)
    nxt = jnp.where(t < nt - 1, next_ref[0].astype(F32), 0.0)
    for blk in range(D_CONV // LANES):
        lanes = slice(blk * LANES, (blk + 1) * LANES)
        win_ref[blk, 0:CONV_HALO, :] = prev[:, lanes]
        win_ref[blk, CONV_HALO:CONV_HALO + tm, :] = cur_ref[0, :, lanes].astype(F32)
        win_ref[blk, CONV_HALO + tm:, :] = nxt[:, lanes]

    rc = CONV_ROW_CHUNK
    first = CONV_HALO - CONV_HALF

    def chunk(c, carry):
        r0 = pl.multiple_of(c * rc, rc)
        for blk in range(D_CONV // LANES):
            lanes = slice(blk * LANES, (blk + 1) * LANES)
            acc = jnp.zeros((rc, LANES), F32)
            for j in range(CONV_WIDTH):
                acc = acc + cw_ref[j:j + 1, lanes] * win_ref[blk, pl.ds(r0 + first + j, rc), :]
            conv_ref[pl.ds(r0, rc), lanes] = acc
        return carry

    lax.fori_loop(0, tm // rc, chunk, 0)
    conv = conv_ref[...] + cb_ref[...]
    mu = jnp.mean(conv, axis=-1, keepdims=True)
    xc = conv - mu
    y = xc * lax.rsqrt(jnp.mean(xc * xc, axis=-1, keepdims=True) + LN_EPS) * lg_ref[...] + lb_ref[...]
    a = (y * jax.nn.sigmoid(y)).astype(BF16)
    cat = jnp.concatenate([a, f_ref[0]], axis=-1)
    o_ref[0] = x_ref[0] + _dot(cat, w_out_ref[...])


def _even_out(x, hglu, f, conv_w, conv_b, ln_g, ln_b, w_out, tm):
    b, s, d = x.shape
    w_spec, w_out = _layer_weight(*w_out)
    per_halo = tm // CONV_HALO
    last_halo = s // CONV_HALO - 1
    tile = lambda width: pl.BlockSpec((1, tm, width), lambda i, j: (i, j, 0))
    prev = pl.BlockSpec((1, CONV_HALO, D_CONV), lambda i, j: (i, jnp.maximum(j * per_halo - 1, 0), 0))
    nxt = pl.BlockSpec((1, CONV_HALO, D_CONV),
                       lambda i, j: (i, jnp.minimum((j + 1) * per_halo, last_halo), 0))
    return pl.pallas_call(
        _even_out_kernel,
        grid=(b, s // tm),
        in_specs=[tile(d), prev, tile(D_CONV), nxt, tile(D_FOURIER),
                  _resident(conv_w.shape), _resident((1, D_CONV)), _resident((1, D_CONV)),
                  _resident((1, D_CONV)), w_spec],
        out_specs=tile(d),
        out_shape=jax.ShapeDtypeStruct((b, s, d), F32),
        scratch_shapes=[pltpu.VMEM((D_CONV // LANES, tm + 2 * CONV_HALO, LANES), F32),
                        pltpu.VMEM((tm, D_CONV), F32)],
        compiler_params=_params(2),
        name="even_out",
    )(x, hglu, hglu, hglu, f, conv_w, conv_b, ln_g, ln_b, w_out)


def _pool_mix(prev, x, nxt, g, w_ref, scale, t, seq_len, win_ref, sum_ref):
    tm = x.shape[0]
    n = tm + 2 * POOL_HALO
    hp, hx, hn = _rms(prev, g), _rms(x, g), _rms(nxt, g)
    for blk in range(D_MODEL // LANES):
        lanes = slice(blk * LANES, (blk + 1) * LANES)
        win_ref[blk, 0:POOL_HALO, :] = hp[:, lanes]
        win_ref[blk, POOL_HALO:POOL_HALO + tm, :] = hx[:, lanes]
        win_ref[blk, POOL_HALO + tm:, :] = hn[:, lanes]

    def inv_count(first_row, w):
        pos = first_row + lax.broadcasted_iota(jnp.int32, (POOL_HALO, 1), 0)
        cnt = jnp.minimum(pos + w // 2, seq_len) - jnp.maximum(pos - w // 2, 0)
        return 1.0 / cnt.astype(F32)

    blocks_per_group = POOL_GROUP_DIM // LANES
    ys = []
    for grp, w in enumerate(POOL_WINDOWS):
        assert w <= 2 * POOL_HALO
        start = POOL_HALO - w // 2
        cols = []
        for blk in range(grp * blocks_per_group, (grp + 1) * blocks_per_group):
            read = lambda lo, rows: win_ref[blk, lo:lo + rows, :]
            width, rows, level, slot = 1, n, None, 0
            while 2 * width <= min(w, POOL_HALO):
                rows -= width
                level = read(0, rows) + read(width, rows)
                width *= 2
                doubles_again = 2 * width <= min(w, POOL_HALO)
                if doubles_again or start % SUBLANES or (w // width > 1 and width % SUBLANES):
                    sum_ref[blk % blocks_per_group, slot, 0:rows, :] = level
                    read = (lambda sl: lambda lo, rows: sum_ref[blk % blocks_per_group, sl, lo:lo + rows, :])(slot)
                    level, slot = None, 1 - slot
            take = (lambda lo: level[lo:lo + tm]) if level is not None else (lambda lo: read(lo, tm))
            tot = take(start)
            for k in range(1, w // width):
                tot = tot + take(start + k * width)
            head, tail = slice(0, POOL_HALO), slice(tm - POOL_HALO, tm)
            mean = jnp.concatenate([tot[head] * inv_count(t * tm, w), tot[POOL_HALO:tm - POOL_HALO] * (1.0 / w),
                                    tot[tail] * inv_count(t * tm + tm - POOL_HALO, w)], axis=0)
            cols.append(mean - win_ref[blk, POOL_HALO:POOL_HALO + tm, :])
        ys.append(_dot(jnp.concatenate(cols, axis=-1).astype(BF16), w_ref[grp]))
    return x + jnp.concatenate(ys, axis=-1) * scale


SCORE_SCALE_LOG2 = float(XA_HEAD_DIM ** -0.5 * np.log2(np.e))


def _project_memory(mem_ref, gm_ref, wkv_ref, k_ref, v_ref):
    @pl.when(pl.program_id(1) == 0)
    def _():
        m = _rms(mem_ref[0], gm_ref[...]).astype(BF16)
        kv = _dot(m, wkv_ref[...])
        k_ref[...] = (kv[:, :D_MODEL] * SCORE_SCALE_LOG2).astype(BF16)
        v_ref[...] = kv[:, D_MODEL:].astype(BF16)


def _attend(x, g, wq_ref, k_ref, v_ref, wo_ref):
    h = _rms(x, g).astype(BF16)
    q = _dot(h, wq_ref[...]).astype(BF16)
    heads = []
    for hd in range(XA_HEADS):
        c0 = hd * XA_HEAD_DIM
        z = lax.dot_general(q[:, c0:c0 + XA_HEAD_DIM], k_ref[:, c0:c0 + XA_HEAD_DIM],
                            (((1,), (1,)), ((), ())), preferred_element_type=F32)
        e = jnp.exp2(z - jnp.max(z, axis=-1, keepdims=True))
        o = _dot(e.astype(BF16), v_ref[:, c0:c0 + XA_HEAD_DIM])
        heads.append((o / jnp.sum(e, axis=-1, keepdims=True)).astype(BF16))
    return x + _dot(jnp.concatenate(heads, axis=-1), wo_ref[...])


def _attn_kernel(x_ref, g_ref, wq_ref, mem_ref, gm_ref, wkv_ref, wo_ref, o_ref, k_ref, v_ref):
    _project_memory(mem_ref, gm_ref, wkv_ref, k_ref, v_ref)
    o_ref[0] = _attend(x_ref[0], g_ref[...], wq_ref, k_ref, v_ref, wo_ref)


def _pool_attn_kernel(prev_ref, x_ref, next_ref, gp_ref, wp_ref, scale_ref, g_ref, wq_ref, mem_ref, gm_ref,
                      wkv_ref, wo_ref, o_ref, k_ref, v_ref, win_ref, sum_ref, *, seq_len):
    t, nt = pl.program_id(1), pl.num_programs(1)
    _project_memory(mem_ref, gm_ref, wkv_ref, k_ref, v_ref)
    prev = jnp.where(t > 0, prev_ref[0], 0.0)
    nxt = jnp.where(t < nt - 1, next_ref[0], 0.0)
    x = _pool_mix(prev, x_ref[0], nxt, gp_ref[...], wp_ref, scale_ref[...], t, seq_len, win_ref, sum_ref)
    o_ref[0] = _attend(x, g_ref[...], wq_ref, k_ref, v_ref, wo_ref)


def _attn(x, g, wq, mem, g_mem, wkv, wo, tm, pool=None):
    b, s, d = x.shape
    m = mem.shape[1]
    tile = pl.BlockSpec((1, tm, d), lambda i, j: (i, j, 0))
    memblk = pl.BlockSpec((1, m, d), lambda i, j: (i, 0, 0))
    (wq_spec, wq), (wkv_spec, wkv), (wo_spec, wo) = _layer_weight(*wq), _layer_weight(*wkv), _layer_weight(*wo)
    attn_specs = [_resident((1, d)), wq_spec, memblk, _resident((1, d)), wkv_spec, wo_spec]
    attn_args = (g, wq, mem, g_mem, wkv, wo)
    scratch = [pltpu.VMEM((m, d), BF16), pltpu.VMEM((m, d), BF16)]
    if pool is None:
        body, specs, args = _attn_kernel, [tile], (x,)
    else:
        per_halo = tm // POOL_HALO
        last_halo = s // POOL_HALO - 1
        prev = pl.BlockSpec((1, POOL_HALO, d), lambda i, j: (i, jnp.maximum(j * per_halo - 1, 0), 0))
        nxt = pl.BlockSpec((1, POOL_HALO, d),
                           lambda i, j: (i, jnp.minimum((j + 1) * per_halo, last_halo), 0))
        gp, wp, scale = pool
        body = functools.partial(_pool_attn_kernel, seq_len=s)
        specs = [prev, tile, nxt, _resident((1, d)), _resident(wp.shape), _resident((1, d))]
        args = (x, x, x, gp, wp, scale)
        rows = tm + 2 * POOL_HALO
        scratch += [pltpu.VMEM((d // LANES, rows, LANES), F32),
                    pltpu.VMEM((POOL_GROUP_DIM // LANES, 2, rows, LANES), F32)]
    return pl.pallas_call(
        body,
        grid=(b, s // tm),
        in_specs=specs + attn_specs,
        out_specs=tile,
        out_shape=jax.ShapeDtypeStruct((b, s, d), F32),
        scratch_shapes=scratch,
        compiler_params=pltpu.CompilerParams(dimension_semantics=("arbitrary", "arbitrary"),
                                             vmem_limit_bytes=VMEM_LIMIT_BYTES),
        name="cross_attn" if pool is None else "pool_cross_attn",
    )(*args, *attn_args)


def _ffn_chunks(d_ff):
    mxu = 256
    assert d_ff % mxu == 0
    n = d_ff // mxu
    sizes = [n // 2, n - n // 2] if n > 4 else [n]
    out, c0 = [], 0
    for sz in sizes:
        out.append((c0, sz * mxu))
        c0 += sz * mxu
    return out


def _ffn_kernel(x_ref, g_ref, wgu_ref, wd_ref, gf_ref, o_ref, *, final_norm):
    d_ff = wd_ref.shape[0]
    x = x_ref[...]
    h = _rms(x, g_ref[...]).astype(BF16)
    acc = x
    for c0, width in _ffn_chunks(d_ff):
        gate = _dot(h, wgu_ref[:, c0:c0 + width])
        up = _dot(h, wgu_ref[:, d_ff + c0:d_ff + c0 + width])
        act = (gate * jax.nn.sigmoid(gate) * up).astype(BF16)
        acc = acc + _dot(act, wd_ref[c0:c0 + width, :])
    o_ref[...] = _rms(acc, gf_ref[...]) if final_norm else acc


def _ffn(x, g, wgu, wd, g_final, tm, final_norm):
    n, d = x.shape
    (wgu_spec, wgu), (wd_spec, wd) = _layer_weight(*wgu), _layer_weight(*wd)
    tile = pl.BlockSpec((tm, d), lambda i: (i, 0))
    return pl.pallas_call(
        functools.partial(_ffn_kernel, final_norm=final_norm),
        grid=(n // tm,),
        in_specs=[tile, _resident((1, d)), wgu_spec, wd_spec, _resident((1, d))],
        out_specs=tile,
        out_shape=jax.ShapeDtypeStruct((n, d), F32),
        compiler_params=_params(1),
        name="ffn",
    )(x, g, wgu, wd, g_final)


def _channel_dft():
    n = FOURIER_GROUP_DIM
    idx = (np.arange(n)[:, None] * np.arange(n)[None, :]) % n
    ang = 2.0 * np.pi * idx / n
    m = np.concatenate([np.cos(ang), np.sin(ang)], axis=1) / np.sqrt(n)
    return jnp.asarray(m, dtype=F32).astype(BF16)


def _sequence_dft_tables(s):
    q = s // DFT_RADIX
    n2 = np.arange(q)
    k1 = np.arange(1, DFT_RADIX)
    ang = ((k1[:, None] * n2[None, :]) % s) * (2.0 * np.pi / s)
    tw = jnp.asarray(np.concatenate([np.cos(ang), np.sin(ang)], axis=0), dtype=F32)
    tw = jnp.broadcast_to(tw[:, :, None], (2 * (DFT_RADIX - 1), q, LANES))
    ang = ((n2[:, None] * n2[None, :]) % q) * (2.0 * np.pi / q)
    m = np.concatenate([np.cos(ang), -np.sin(ang)], axis=1) * (s ** -0.5)
    return tw, jnp.asarray(m, dtype=F32).astype(BF16)


def _trunk(x, mem, p, tm=1024, tm_attn=1024, tm_ffn=1024):
    b, s, d = x.shape
    depth = p["norm_mix"].shape[0]
    tm = _seq_tile(s, tm)
    tm_attn = _seq_tile(s, tm_attn)
    row = lambda v: v.reshape(1, -1)
    for l in range(depth):
        if l % 2 == 0:
            e = l // 2
            hglu, fa, fb = _even_in(x, row(p["norm_mix"][l]), (p["w_in_even"], e), p["cdft"], tm)
            f = _seq_dft(_sequence_dft_tables(s), fa, fb)
            x = _even_out(x, hglu, f, p["conv_w"][e], row(p["conv_b"][e]), row(p["conv_ln_g"][e]),
                          row(p["conv_ln_b"][e]), (p["w_out_even"], e), tm)
            pool = None
        else:
            o = l // 2
            pool = (row(p["norm_mix"][l]), p["w_pool"][o], row(p["pool_scale"][o]))
        x = _attn(x, row(p["norm_xa"][l]), (p["xa_wq"], l), mem, row(p["norm_mem"][l]), (p["xa_wkv"], l),
                  (p["xa_wo"], l), tm_attn, pool)
        x = _ffn(x.reshape(b * s, d), row(p["norm_ffn"][l]), (p["ffn_w_gate_up"], l),
                 (p["ffn_w_down"], l), row(p["norm_final"]), _seq_tile(b * s, tm_ffn),
                 final_norm=(l == depth - 1)).reshape(b, s, d)
    return x


def _prepare(norm_mix, w_in_even, conv_w, conv_b, conv_ln_g, conv_ln_b, w_out_even, w_pool, pool_scale,
             norm_xa, norm_mem, xa_wq, xa_wkv, xa_wo, norm_ffn, ffn_w_gate_up, ffn_w_down, norm_final):
    conv_w = jnp.pad(conv_w, ((0, 0), (0, -conv_w.shape[1] % SUBLANES), (0, 0)))
    return dict(
        norm_mix=norm_mix, w_in_even=_to_bf16(w_in_even), conv_w=conv_w, conv_b=conv_b,
        conv_ln_g=conv_ln_g, conv_ln_b=conv_ln_b, w_out_even=_to_bf16(w_out_even),
        w_pool=w_pool.astype(BF16), pool_scale=pool_scale, norm_xa=norm_xa, norm_mem=norm_mem,
        xa_wq=_to_bf16(xa_wq), xa_wkv=_to_bf16(xa_wkv), xa_wo=_to_bf16(xa_wo),
        norm_ffn=norm_ffn, ffn_w_gate_up=_to_bf16(ffn_w_gate_up),
        ffn_w_down=_to_bf16(ffn_w_down), norm_final=norm_final, cdft=_channel_dft())


def kernel(x_prompt, x_sample, mem_prompt, mem_sample, norm_mix, w_in_even, conv_w, conv_b, conv_ln_g,
           conv_ln_b, w_out_even, w_pool, pool_scale, norm_xa, norm_mem, xa_wq, xa_wkv, xa_wo, norm_ffn,
           ffn_w_gate_up, ffn_w_down, norm_final):
    p = _prepare(norm_mix, w_in_even, conv_w, conv_b, conv_ln_g, conv_ln_b, w_out_even, w_pool,
                 pool_scale, norm_xa, norm_mem, xa_wq, xa_wkv, xa_wo, norm_ffn, ffn_w_gate_up,
                 ffn_w_down, norm_final)
    return _trunk(x_prompt, mem_prompt, p), _trunk(x_sample, mem_sample, p)
```

```python
import functools

import jax
import jax.numpy as jnp
import numpy as np
from jax import lax
from jax.experimental import pallas as pl
from jax.experimental.pallas import tpu as pltpu

D_MODEL = 1024
XA_HEADS = 4
XA_HEAD_DIM = D_MODEL // XA_HEADS
CONV_WIDTH = 31
CONV_HALF = CONV_WIDTH // 2
D_CONV = D_MODEL // 2
D_FOURIER = D_MODEL - D_CONV
FOURIER_GROUPS = 4
FOURIER_GROUP_DIM = D_FOURIER // FOURIER_GROUPS
POOL_WINDOWS = (2, 4, 8, 16)
POOL_GROUP_DIM = D_MODEL // len(POOL_WINDOWS)
RMS_EPS = 1e-6
LN_EPS = 1e-5

SUBLANES = 8
LANES = 128
BF16_ROWS = 16
CONV_HALO = 16
POOL_HALO = 8
CONV_ROW_CHUNK = 128
VMEM_LIMIT_BYTES = 56 * 1024 * 1024
CAST_ROW_TILE = 256
DFT_RADIX = 4
DFT_ROW_CHUNK = 32
DFT_ROW_TILE = 512

BF16 = jnp.bfloat16
F32 = jnp.float32


def _seq_tile(seq_len, want):
    tile = min(want, seq_len)
    assert seq_len % tile == 0 and tile % CONV_ROW_CHUNK == 0
    return tile


def _params(n_axes):
    return pltpu.CompilerParams(dimension_semantics=("parallel",) * n_axes,
                                vmem_limit_bytes=VMEM_LIMIT_BYTES)


def _resident(shape):
    zeros = (0,) * len(shape)
    return pl.BlockSpec(shape, lambda *_: zeros, pipeline_mode=pl.Buffered(1))


def _layer_weight(stack, layer):
    zeros = (0,) * (stack.ndim - 1)
    spec = pl.BlockSpec((None,) + stack.shape[1:], lambda *_: (layer,) + zeros, pipeline_mode=pl.Buffered(1))
    return spec, stack


def _cast_kernel(w_ref, o_ref):
    o_ref[...] = w_ref[...].astype(BF16)


def _to_bf16(w):
    layers, rows, cols = w.shape
    tr = CAST_ROW_TILE
    assert rows % tr == 0
    blk = pl.BlockSpec((1, tr, cols), lambda i, j: (i, j, 0))
    return pl.pallas_call(
        _cast_kernel,
        grid=(layers, rows // tr),
        in_specs=[blk],
        out_specs=blk,
        out_shape=jax.ShapeDtypeStruct(w.shape, BF16),
        compiler_params=_params(2),
        name="cast_bf16",
    )(w)


def _rms(xf, g):
    return xf * lax.rsqrt(jnp.mean(xf * xf, axis=-1, keepdims=True) + RMS_EPS) * g


def _dot(a, b):
    return jnp.dot(a, b, preferred_element_type=F32)


def _even_in_kernel(x_ref, g_ref, w_in_ref, cdft_ref, hglu_ref, fa_ref, fb_ref):
    h = _rms(x_ref[0], g_ref[...]).astype(BF16)
    u = _dot(h, w_in_ref[...])
    val, gate = u[:, :D_CONV], u[:, D_CONV:2 * D_CONV]
    hglu_ref[0] = (val * jax.nn.sigmoid(gate)).astype(BF16)
    uf = u[:, 2 * D_CONV:].astype(BF16)
    fa, fb = [], []
    for grp in range(FOURIER_GROUPS):
        c0 = grp * FOURIER_GROUP_DIM
        ab = _dot(uf[:, c0:c0 + FOURIER_GROUP_DIM], cdft_ref[...])
        fa.append(ab[:, :FOURIER_GROUP_DIM])
        fb.append(ab[:, FOURIER_GROUP_DIM:])
    fa_ref[0] = jnp.concatenate(fa, axis=-1).astype(BF16)
    fb_ref[0] = jnp.concatenate(fb, axis=-1).astype(BF16)


def _even_in(x, g, w_in, cdft, tm):
    b, s, d = x.shape
    w_spec, w_in = _layer_weight(*w_in)
    tile = lambda width: pl.BlockSpec((1, tm, width), lambda i, j: (i, j, 0))
    out = jax.ShapeDtypeStruct((b, s, D_CONV), BF16)
    return pl.pallas_call(
        _even_in_kernel,
        grid=(b, s // tm),
        in_specs=[tile(d), _resident((1, d)), w_spec,
                  _resident((FOURIER_GROUP_DIM, 2 * FOURIER_GROUP_DIM))],
        out_specs=[tile(D_CONV), tile(D_FOURIER), tile(D_FOURIER)],
        out_shape=[out, out, out],
        compiler_params=_params(2),
        name="even_in",
    )(x, g, w_in, cdft)


def _seq_dft_kernel(a_ref, b_ref, tw_ref, m_ref, f_ref, z_ref, nat_ref):
    s, c = a_ref.shape[1], a_ref.shape[2]
    q = s // DFT_RADIX
    tk = f_ref.shape[1] // DFT_RADIX
    rb = DFT_ROW_CHUNK

    @pl.when(pl.program_id(1) == 0)
    def _():
        def chunk(i, carry):
            r0 = pl.multiple_of(i * rb, rb)
            for l0 in range(0, c, LANES):
                lanes = slice(l0, l0 + LANES)
                a = [a_ref[0, pl.ds(n1 * q + r0, rb), lanes].astype(F32) for n1 in range(DFT_RADIX)]
                b = [b_ref[0, pl.ds(n1 * q + r0, rb), lanes].astype(F32) for n1 in range(DFT_RADIX)]
                sa02, da02, sa13, da13 = a[0] + a[2], a[0] - a[2], a[1] + a[3], a[1] - a[3]
                sb02, db02, sb13, db13 = b[0] + b[2], b[0] - b[2], b[1] + b[3], b[1] - b[3]
                re = [sa02 + sa13, da02 - db13, sa02 - sa13, da02 + db13]
                neg = [sb02 + sb13, db02 + da13, sb02 - sb13, db02 - da13]
                for k1 in range(DFT_RADIX):
                    zr, zn = re[k1], neg[k1]
                    if k1:
                        cs = tw_ref[k1 - 1, pl.ds(r0, rb), :]
                        sn = tw_ref[DFT_RADIX - 1 + k1 - 1, pl.ds(r0, rb), :]
                        zr, zn = cs * re[k1] - sn * neg[k1], cs * neg[k1] + sn * re[k1]
                    z_ref[k1, pl.ds(r0, rb), lanes] = zr.astype(BF16)
                    z_ref[k1, pl.ds(q + r0, rb), lanes] = zn.astype(BF16)
            return carry

        lax.fori_loop(0, q // rb, chunk, 0)

    rows = m_ref[pl.ds(pl.multiple_of(pl.program_id(1) * tk, tk), tk), :]
    for k1 in range(DFT_RADIX):
        res = _dot(rows, z_ref[k1])
        for blk in range(c // LANES):
            nat_ref[blk, pl.ds(k1, tk, stride=DFT_RADIX), :] = res[:, blk * LANES:(blk + 1) * LANES]
    for blk in range(c // LANES):
        f_ref[0, :, blk * LANES:(blk + 1) * LANES] = nat_ref[blk].astype(BF16)


def _seq_dft(tables, fa, fb):
    tw, m = tables
    b, s, c = fa.shape
    q = s // DFT_RADIX
    tk = min(q, DFT_ROW_TILE)
    full = pl.BlockSpec((1, s, c), lambda i, j: (i, 0, 0))
    return pl.pallas_call(
        _seq_dft_kernel,
        grid=(b, q // tk),
        in_specs=[full, full, _resident(tw.shape), _resident(m.shape)],
        out_specs=pl.BlockSpec((1, DFT_RADIX * tk, c), lambda i, j: (i, j, 0)),
        out_shape=jax.ShapeDtypeStruct((b, s, c), BF16),
        scratch_shapes=[pltpu.VMEM((DFT_RADIX, 2 * q, c), BF16),
                        pltpu.VMEM((c // LANES, DFT_RADIX * tk, LANES), F32)],
        compiler_params=pltpu.CompilerParams(dimension_semantics=("arbitrary", "arbitrary"),
                                             vmem_limit_bytes=VMEM_LIMIT_BYTES),
        name="seq_dft",
    )(fa, fb, tw, m)


def _even_out_kernel(x_ref, prev_ref, cur_ref, next_ref, f_ref, cw_ref, cb_ref, lg_ref, lb_ref,
                     w_out_ref, o_ref, win_ref, conv_ref):
    t, nt = pl.program_id(1), pl.num_programs(1)
    tm = cur_ref.shape[1]
    prev = jnp.where(t > 0, prev_ref[0].astype(F32), 0.0)
    nxt = jnp.where(t < nt - 1, next_ref[0].astype(F32), 0.0)
    for blk in range(D_CONV // LANES):
        lanes = slice(blk * LANES, (blk + 1) * LANES)
        win_ref[blk, 0:CONV_HALO, :] = prev[:, lanes]
        win_ref[blk, CONV_HALO:CONV_HALO + tm, :] = cur_ref[0, :, lanes].astype(F32)
        win_ref[blk, CONV_HALO + tm:, :] = nxt[:, lanes]

    rc = CONV_ROW_CHUNK
    first = CONV_HALO - CONV_HALF

    def chunk(c, carry):
        r0 = pl.multiple_of(c * rc, rc)
        for blk in range(D_CONV // LANES):
            lanes = slice(blk * LANES, (blk + 1) * LANES)
            acc = jnp.zeros((rc, LANES), F32)
            for j in range(CONV_WIDTH):
                acc = acc + cw_ref[j:j + 1, lanes] * win_ref[blk, pl.ds(r0 + first + j, rc), :]
            conv_ref[pl.ds(r0, rc), lanes] = acc
        return carry

    lax.fori_loop(0, tm // rc, chunk, 0)
    conv = conv_ref[...] + cb_ref[...]
    mu = jnp.mean(conv, axis=-1, keepdims=True)
    xc = conv - mu
    y = xc * lax.rsqrt(jnp.mean(xc * xc, axis=-1, keepdims=True) + LN_EPS) * lg_ref[...] + lb_ref[...]
    a = (y * jax.nn.sigmoid(y)).astype(BF16)
    cat = jnp.concatenate([a, f_ref[0]], axis=-1)
    o_ref[0] = x_ref[0] + _dot(cat, w_out_ref[...])


def _even_out(x, hglu, f, conv_w, conv_b, ln_g, ln_b, w_out, tm):
    b, s, d = x.shape
    w_spec, w_out = _layer_weight(*w_out)
    per_halo = tm // CONV_HALO
    last_halo = s // CONV_HALO - 1
    tile = lambda width: pl.BlockSpec((1, tm, width), lambda i, j: (i, j, 0))
    prev = pl.BlockSpec((1, CONV_HALO, D_CONV), lambda i, j: (i, jnp.maximum(j * per_halo - 1, 0), 0))
    nxt = pl.BlockSpec((1, CONV_HALO, D_CONV),
                       lambda i, j: (i, jnp.minimum((j + 1) * per_halo, last_halo), 0))
    return pl.pallas_call(
        _even_out_kernel,
        grid=(b, s // tm),
        in_specs=[tile(d), prev, tile(D_CONV), nxt, tile(D_FOURIER),
                  _resident(conv_w.shape), _resident((1, D_CONV)), _resident((1, D_CONV)),
                  _resident((1, D_CONV)), w_spec],
        out_specs=tile(d),
        out_shape=jax.ShapeDtypeStruct((b, s, d), F32),
        scratch_shapes=[pltpu.VMEM((D_CONV // LANES, tm + 2 * CONV_HALO, LANES), F32),
                        pltpu.VMEM((tm, D_CONV), F32)],
        compiler_params=_params(2),
        name="even_out",
    )(x, hglu, hglu, hglu, f, conv_w, conv_b, ln_g, ln_b, w_out)


def _pool_mix(prev, x, nxt, g, w_ref, scale, t, seq_len):
    tm = x.shape[0]
    n = tm + 2 * POOL_HALO
    hwin = _rms(jnp.concatenate([prev, x, nxt], axis=0), g)

    def inv_count(first_row, w):
        pos = first_row + lax.broadcasted_iota(jnp.int32, (POOL_HALO, 1), 0)
        cnt = jnp.minimum(pos + w // 2, seq_len) - jnp.maximum(pos - w // 2, 0)
        return 1.0 / cnt.astype(F32)

    def ahead(v, k):
        return pltpu.roll(v, n - k, axis=0)

    def rows_from(v, k):
        return v[k:k + tm] if k % SUBLANES == 0 else ahead(v, k)[:tm]

    ys = []
    for grp, w in enumerate(POOL_WINDOWS):
        assert w <= 2 * POOL_HALO
        c0 = grp * POOL_GROUP_DIM
        hg = hwin[:, c0:c0 + POOL_GROUP_DIM]
        acc, width = hg, 1
        while 2 * width <= min(w, POOL_HALO):
            acc = acc + ahead(acc, width)
            width *= 2
        tot = rows_from(acc, POOL_HALO - w // 2)
        for k in range(1, w // width):
            tot = tot + rows_from(acc, POOL_HALO - w // 2 + k * width)
        head, tail = slice(0, POOL_HALO), slice(tm - POOL_HALO, tm)
        mean = jnp.concatenate([tot[head] * inv_count(t * tm, w), tot[POOL_HALO:tm - POOL_HALO] * (1.0 / w),
                                tot[tail] * inv_count(t * tm + tm - POOL_HALO, w)], axis=0)
        p = mean - hg[POOL_HALO:POOL_HALO + tm]
        ys.append(_dot(p.astype(BF16), w_ref[grp]))
    return x + jnp.concatenate(ys, axis=-1) * scale


SCORE_SCALE_LOG2 = float(XA_HEAD_DIM ** -0.5 * np.log2(np.e))


def _kv_kernel(mem_ref, g_ref, wkv_ref, k_ref, v_ref):
    m = _rms(mem_ref[...], g_ref[...]).astype(BF16)
    kv = _dot(m, wkv_ref[...])
    k_ref[...] = (kv[:, :D_MODEL] * SCORE_SCALE_LOG2).astype(BF16)
    v_ref[...] = kv[:, D_MODEL:].astype(BF16)


def _kv(mem, g, wkv, tm):
    b, m, d = mem.shape
    w_spec, wkv = _layer_weight(*wkv)
    tm = _seq_tile(b * m, tm)
    blk = pl.BlockSpec((tm, d), lambda i: (i, 0))
    out = jax.ShapeDtypeStruct((b * m, d), BF16)
    k, v = pl.pallas_call(
        _kv_kernel,
        grid=(b * m // tm,),
        in_specs=[blk, _resident((1, d)), w_spec],
        out_specs=[blk, blk],
        out_shape=[out, out],
        compiler_params=_params(1),
        name="mem_kv",
    )(mem.reshape(b * m, d), g, wkv)
    return k.reshape(b, m, d), v.reshape(b, m, d)


def _attend(x, g, wq_ref, k_ref, v_ref, wo_ref):
    h = _rms(x, g).astype(BF16)
    q = _dot(h, wq_ref[...]).astype(BF16)
    heads = []
    for hd in range(XA_HEADS):
        c0 = hd * XA_HEAD_DIM
        z = lax.dot_general(q[:, c0:c0 + XA_HEAD_DIM], k_ref[0, :, c0:c0 + XA_HEAD_DIM],
                            (((1,), (1,)), ((), ())), preferred_element_type=F32)
        e = jnp.exp2(z - jnp.max(z, axis=-1, keepdims=True))
        o = _dot(e.astype(BF16), v_ref[0, :, c0:c0 + XA_HEAD_DIM])
        heads.append((o / jnp.sum(e, axis=-1, keepdims=True)).astype(BF16))
    return x + _dot(jnp.concatenate(heads, axis=-1), wo_ref[...])


def _attn_kernel(x_ref, g_ref, wq_ref, k_ref, v_ref, wo_ref, o_ref):
    o_ref[0] = _attend(x_ref[0], g_ref[...], wq_ref, k_ref, v_ref, wo_ref)


def _pool_attn_kernel(prev_ref, x_ref, next_ref, gp_ref, wp_ref, scale_ref, g_ref, wq_ref, k_ref, v_ref,
                      wo_ref, o_ref, *, seq_len):
    t, nt = pl.program_id(1), pl.num_programs(1)
    prev = jnp.where(t > 0, prev_ref[0], 0.0)
    nxt = jnp.where(t < nt - 1, next_ref[0], 0.0)
    x = _pool_mix(prev, x_ref[0], nxt, gp_ref[...], wp_ref, scale_ref[...], t, seq_len)
    o_ref[0] = _attend(x, g_ref[...], wq_ref, k_ref, v_ref, wo_ref)


def _attn(x, g, wq, k, v, wo, tm, pool=None):
    b, s, d = x.shape
    m = k.shape[1]
    tile = pl.BlockSpec((1, tm, d), lambda i, j: (i, j, 0))
    memblk = pl.BlockSpec((1, m, d), lambda i, j: (i, 0, 0))
    (wq_spec, wq), (wo_spec, wo) = _layer_weight(*wq), _layer_weight(*wo)
    attn_specs = [_resident((1, d)), wq_spec, memblk, memblk, wo_spec]
    attn_args = (g, wq, k, v, wo)
    if pool is None:
        body, specs, args = _attn_kernel, [tile], (x,)
    else:
        per_halo = tm // POOL_HALO
        last_halo = s // POOL_HALO - 1
        prev = pl.BlockSpec((1, POOL_HALO, d), lambda i, j: (i, jnp.maximum(j * per_halo - 1, 0), 0))
        nxt = pl.BlockSpec((1, POOL_HALO, d),
                           lambda i, j: (i, jnp.minimum((j + 1) * per_halo, last_halo), 0))
        gp, wp, scale = pool
        body = functools.partial(_pool_attn_kernel, seq_len=s)
        specs = [prev, tile, nxt, _resident((1, d)), _resident(wp.shape), _resident((1, d))]
        args = (x, x, x, gp, wp, scale)
    return pl.pallas_call(
        body,
        grid=(b, s // tm),
        in_specs=specs + attn_specs,
        out_specs=tile,
        out_shape=jax.ShapeDtypeStruct((b, s, d), F32),
        compiler_params=_params(2),
        name="cross_attn" if pool is None else "pool_cross_attn",
    )(*args, *attn_args)


def _ffn_chunks(d_ff):
    mxu = 256
    assert d_ff % mxu == 0
    n = d_ff // mxu
    sizes = [n // 2, n - n // 2] if n > 4 else [n]
    out, c0 = [], 0
    for sz in sizes:
        out.append((c0, sz * mxu))
        c0 += sz * mxu
    return out


def _ffn_kernel(x_ref, g_ref, wgu_ref, wd_ref, gf_ref, o_ref, *, final_norm):
    d_ff = wd_ref.shape[0]
    x = x_ref[...]
    h = _rms(x, g_ref[...]).astype(BF16)
    acc = x
    for c0, width in _ffn_chunks(d_ff):
        gate = _dot(h, wgu_ref[:, c0:c0 + width])
        up = _dot(h, wgu_ref[:, d_ff + c0:d_ff + c0 + width])
        act = (gate * jax.nn.sigmoid(gate) * up).astype(BF16)
        acc = acc + _dot(act, wd_ref[c0:c0 + width, :])
    o_ref[...] = _rms(acc, gf_ref[...]) if final_norm else acc


def _ffn(x, g, wgu, wd, g_final, tm, final_norm):
    n, d = x.shape
    (wgu_spec, wgu), (wd_spec, wd) = _layer_weight(*wgu), _layer_weight(*wd)
    tile = pl.BlockSpec((tm, d), lambda i: (i, 0))
    return pl.pallas_call(
        functools.partial(_ffn_kernel, final_norm=final_norm),
        grid=(n // tm,),
        in_specs=[tile, _resident((1, d)), wgu_spec, wd_spec, _resident((1, d))],
        out_specs=tile,
        out_shape=jax.ShapeDtypeStruct((n, d), F32),
        compiler_params=_params(1),
        name="ffn",
    )(x, g, wgu, wd, g_final)


def _channel_dft():
    n = FOURIER_GROUP_DIM
    idx = (np.arange(n)[:, None] * np.arange(n)[None, :]) % n
    ang = 2.0 * np.pi * idx / n
    m = np.concatenate([np.cos(ang), np.sin(ang)], axis=1) / np.sqrt(n)
    return jnp.asarray(m, dtype=F32).astype(BF16)


def _sequence_dft_tables(s):
    q = s // DFT_RADIX
    n2 = np.arange(q)
    k1 = np.arange(1, DFT_RADIX)
    ang = ((k1[:, None] * n2[None, :]) % s) * (2.0 * np.pi / s)
    tw = jnp.asarray(np.concatenate([np.cos(ang), np.sin(ang)], axis=0), dtype=F32)
    tw = jnp.broadcast_to(tw[:, :, None], (2 * (DFT_RADIX - 1), q, LANES))
    ang = ((n2[:, None] * n2[None, :]) % q) * (2.0 * np.pi / q)
    m = np.concatenate([np.cos(ang), -np.sin(ang)], axis=1) * (s ** -0.5)
    return tw, jnp.asarray(m, dtype=F32).astype(BF16)


def _trunk(x, mem, p, tm=1024, tm_attn=1024, tm_ffn=1024):
    b, s, d = x.shape
    depth = p["norm_mix"].shape[0]
    tm = _seq_tile(s, tm)
    tm_attn = _seq_tile(s, tm_attn)
    row = lambda v: v.reshape(1, -1)
    for l in range(depth):
        if l % 2 == 0:
            e = l // 2
            hglu, fa, fb = _even_in(x, row(p["norm_mix"][l]), (p["w_in_even"], e), p["cdft"], tm)
            f = _seq_dft(_sequence_dft_tables(s), fa, fb)
            x = _even_out(x, hglu, f, p["conv_w"][e], row(p["conv_b"][e]), row(p["conv_ln_g"][e]),
                          row(p["conv_ln_b"][e]), (p["w_out_even"], e), tm)
            pool = None
        else:
            o = l // 2
            pool = (row(p["norm_mix"][l]), p["w_pool"][o], row(p["pool_scale"][o]))
        k, v = _kv(mem, row(p["norm_mem"][l]), (p["xa_wkv"], l), tm_attn)
        x = _attn(x, row(p["norm_xa"][l]), (p["xa_wq"], l), k, v, (p["xa_wo"], l), tm_attn, pool)
        x = _ffn(x.reshape(b * s, d), row(p["norm_ffn"][l]), (p["ffn_w_gate_up"], l),
                 (p["ffn_w_down"], l), row(p["norm_final"]), _seq_tile(b * s, tm_ffn),
                 final_norm=(l == depth - 1)).reshape(b, s, d)
    return x


def _prepare(norm_mix, w_in_even, conv_w, conv_b, conv_ln_g, conv_ln_b, w_out_even, w_pool, pool_scale,
             norm_xa, norm_mem, xa_wq, xa_wkv, xa_wo, norm_ffn, ffn_w_gate_up, ffn_w_down, norm_final):
    conv_w = jnp.pad(conv_w, ((0, 0), (0, -conv_w.shape[1] % SUBLANES), (0, 0)))
    return dict(
        norm_mix=norm_mix, w_in_even=_to_bf16(w_in_even), conv_w=conv_w, conv_b=conv_b,
        conv_ln_g=conv_ln_g, conv_ln_b=conv_ln_b, w_out_even=_to_bf16(w_out_even),
        w_pool=w_pool.astype(BF16), pool_scale=pool_scale, norm_xa=norm_xa, norm_mem=norm_mem,
        xa_wq=_to_bf16(xa_wq), xa_wkv=_to_bf16(xa_wkv), xa_wo=_to_bf16(xa_wo),
        norm_ffn=norm_ffn, ffn_w_gate_up=_to_bf16(ffn_w_gate_up),
        ffn_w_down=_to_bf16(ffn_w_down), norm_final=norm_final, cdft=_channel_dft())


def kernel(x_prompt, x_sample, mem_prompt, mem_sample, norm_mix, w_in_even, conv_w, conv_b, conv_ln_g,
           conv_ln_b, w_out_even, w_pool, pool_scale, norm_xa, norm_mem, xa_wq, xa_wkv, xa_wo, norm_ffn,
           ffn_w_gate_up, ffn_w_down, norm_final):
    p = _prepare(norm_mix, w_in_even, conv_w, conv_b, conv_ln_g, conv_ln_b, w_out_even, w_pool,
                 pool_scale, norm_xa, norm_mem, xa_wq, xa_wkv, xa_wo, norm_ffn, ffn_w_gate_up,
                 ffn_w_down, norm_final)
    return _trunk(x_prompt, mem_prompt, p), _trunk(x_sample, mem_sample, p)
```
